```python
import math
import jax, jax.numpy as jnp
from jax import lax
import numpy as np

D_MODEL = 1024
BATCH = 32
SEQ = 2048
DEPTH = 2
DEC_BATCH = 16
DEC_SEQ = 32
PAST_LEN = 4096

CHUNK = 64
N_MIXERS = 2
N_A_LAYERS = (DEPTH + 1) // 2
N_B_LAYERS = DEPTH // 2
LRU_WIDTH = D_MODEL
LRU_HEADS = 8
LRU_BLOCK = LRU_WIDTH // LRU_HEADS
LRU_CONV_WIDTH = 4
LRU_C = 8.0
CF_CONV_WIDTH = 31
PEER_HEADS = 8
PEER_NKEYS = 128
PEER_NEXPERTS = PEER_NKEYS * PEER_NKEYS
PEER_DK = 256
PEER_TOPK = 16
PEER_BLOCK = 256
EPS = 1e-6

kernel_name = "hybrid_rglru_conformer_peer_stream"


def rmsnorm(x, g):
    xf = x.astype(jnp.float32)
    y = xf * lax.rsqrt(jnp.mean(xf * xf, axis=-1, keepdims=True) + EPS)
    return (y * g.astype(jnp.float32)).astype(x.dtype)


def layernorm(x, g, b):
    xf = x.astype(jnp.float32)
    mu = jnp.mean(xf, axis=-1, keepdims=True)
    var = jnp.mean(jnp.square(xf - mu), axis=-1, keepdims=True)
    y = (xf - mu) * lax.rsqrt(var + EPS)
    return (y * g.astype(jnp.float32) + b.astype(jnp.float32)).astype(x.dtype)


def modulate(h, shift, scale):
    return h * (1 + scale[:, None, :]) + shift[:, None, :]


def causal_dwconv(xp, w, b):
    c = xp.shape[-1]
    y = lax.conv_general_dilated(xp, w[:, None, :].astype(xp.dtype), window_strides=(1,), padding="VALID",
                                 dimension_numbers=("NWC", "WIO", "NWC"), feature_group_count=c)
    return y + b


def _lin_combine(c1, c2):
    a1, b1 = c1
    a2, b2 = c2
    return a1 * a2, a2 * b1 + b2


def rglru_block(h, h0, conv_buf, w_in, b_in, conv_w, conv_b, ga_w, ga_b, gx_w, gx_b, lam, w_out, b_out):
    bsz, t, _ = h.shape
    proj = h @ w_in + b_in
    gate_branch, rec_branch = jnp.split(proj, 2, axis=-1)
    gate = jax.nn.gelu(gate_branch)
    xp = jnp.concatenate([conv_buf.astype(rec_branch.dtype), rec_branch], axis=1)
    new_buf = xp[:, -(LRU_CONV_WIDTH - 1):]
    xc = causal_dwconv(xp, conv_w, conv_b)
    xh = xc.reshape(bsz, t, LRU_HEADS, LRU_BLOCK)
    r = jax.nn.sigmoid(jnp.einsum("bthi,hij->bthj", xh, ga_w).reshape(bsz, t, LRU_WIDTH) + ga_b)
    i = jax.nn.sigmoid(jnp.einsum("bthi,hij->bthj", xh, gx_w).reshape(bsz, t, LRU_WIDTH) + gx_b)
    log_a = -LRU_C * r.astype(jnp.float32) * jax.nn.softplus(-lam.astype(jnp.float32))
    a = jnp.exp(log_a)
    mult = jnp.sqrt(-jnp.expm1(2.0 * log_a))
    bvals = mult * (i.astype(jnp.float32) * xc.astype(jnp.float32))
    bvals = bvals.at[:, 0].add(a[:, 0] * h0.astype(jnp.float32))
    _, hs = lax.associative_scan(_lin_combine, (a, bvals), axis=1)
    new_h = hs[:, -1].astype(h0.dtype)
    out = (hs.astype(h.dtype) * gate) @ w_out + b_out
    return out, new_h, new_buf


def conformer_conv_block(h, buf, w_pw1, b_pw1, dw_w, dw_b, ln_g, ln_b, w_pw2, b_pw2):
    proj = h @ w_pw1 + b_pw1
    val, gt = jnp.split(proj, 2, axis=-1)
    glu = val * jax.nn.sigmoid(gt)
    xp = jnp.concatenate([buf.astype(glu.dtype), glu], axis=1)
    new_buf = xp[:, -(CF_CONV_WIDTH - 1):]
    d = causal_dwconv(xp, dw_w, dw_b)
    d = jax.nn.silu(layernorm(d, ln_g, ln_b))
    out = d @ w_pw2 + b_pw2
    return out, new_buf


def peer(h, w_q, k1, k2, u, v):
    n = h.shape[0]
    nb = -(-n // PEER_BLOCK)
    hp = jnp.pad(h, ((0, nb * PEER_BLOCK - n), (0, 0))).reshape(nb, PEER_BLOCK, D_MODEL)
    half = PEER_DK // 2

    def block(hb):
        q = (hb @ w_q).reshape(PEER_BLOCK, PEER_HEADS, PEER_DK).astype(jnp.float32)
        s1 = jnp.einsum("thd,kd->thk", q[..., :half], k1.astype(jnp.float32))
        s2 = jnp.einsum("thd,kd->thk", q[..., half:], k2.astype(jnp.float32))
        s1v, s1i = lax.top_k(s1, PEER_TOPK)
        s2v, s2i = lax.top_k(s2, PEER_TOPK)
        cand = (s1v[..., :, None] + s2v[..., None, :]).reshape(PEER_BLOCK, PEER_HEADS, PEER_TOPK * PEER_TOPK)
        sv, si = lax.top_k(cand, PEER_TOPK)
        idx1 = jnp.take_along_axis(s1i, si // PEER_TOPK, axis=-1)
        idx2 = jnp.take_along_axis(s2i, si % PEER_TOPK, axis=-1)
        e = idx1 * PEER_NKEYS + idx2
        g = jax.nn.softmax(sv, axis=-1).astype(hb.dtype)
        ue = jnp.take(u, e, axis=0)
        ve = jnp.take(v, e, axis=0)
        z = jax.nn.gelu(jnp.einsum("thkd,td->thk", ue, hb))
        return jnp.einsum("thk,thkd->td", g * z, ve)

    return lax.map(block, hp).reshape(nb * PEER_BLOCK, D_MODEL)[:n]


def trunk(x, c, lru_h, lru_conv, dwconv, p):
    new_h, new_conv, new_dw = [], [], []
    for l in range(DEPTH):
        mod = jax.nn.silu(c) @ p["ada_w"][l] + p["ada_b"][l]
        sh1, sc1, g1, sh2, sc2, g2 = jnp.split(mod, 6, axis=-1)
        h = modulate(rmsnorm(x, p["norm_mix"][l]), sh1, sc1)
        j = l // N_MIXERS
        if l % N_MIXERS == 0:
            out, hn, cn = rglru_block(h, lru_h[j], lru_conv[j], p["lru_w_in"][j], p["lru_b_in"][j],
                                      p["lru_conv_w"][j], p["lru_conv_b"][j], p["lru_gate_a_w"][j],
                                      p["lru_gate_a_b"][j], p["lru_gate_x_w"][j], p["lru_gate_x_b"][j],
                                      p["lru_lambda"][j], p["lru_w_out"][j], p["lru_b_out"][j])
            new_h.append(hn)
            new_conv.append(cn)
        else:
            out, dn = conformer_conv_block(h, dwconv[j], p["cf_w_pw1"][j], p["cf_b_pw1"][j], p["cf_dw_w"][j],
                                           p["cf_dw_b"][j], p["cf_ln_g"][j], p["cf_ln_b"][j],
                                           p["cf_w_pw2"][j], p["cf_b_pw2"][j])
            new_dw.append(dn)
        x = x + g1[:, None, :] * out
        h = modulate(rmsnorm(x, p["norm_ffn"][l]), sh2, sc2)
        bsz, t, _ = h.shape
        ffn = peer(h.reshape(bsz * t, D_MODEL), p["peer_w_q"][l], p["peer_k1"][l], p["peer_k2"][l],
                   p["peer_u"][l], p["peer_v"][l]).reshape(bsz, t, D_MODEL)
        x = x + g2[:, None, :] * ffn
    y = rmsnorm(x, p["norm_final"])
    return y, jnp.stack(new_h), jnp.stack(new_conv), jnp.stack(new_dw)


def setup_inputs(seed: int = 0) -> dict:
    key = jax.random.key(seed)
    ks = iter(jax.random.split(key, 40))

    def nrm(shape, scale):
        return jax.random.normal(next(ks), shape, jnp.float32) * scale

    d, w = D_MODEL, LRU_WIDTH
    a_c = jax.random.uniform(next(ks), (N_A_LAYERS, w), jnp.float32, 0.9, 0.999)
    a0 = a_c ** (1.0 / LRU_C)
    lam = jnp.log(a0) - jnp.log1p(-a0)
    return {
        "x_prompt": nrm((BATCH, SEQ, d), 1.0),
        "x_sample": nrm((DEC_BATCH, DEC_SEQ, d), 1.0),
        "c_prompt": nrm((BATCH, d), 1.0),
        "c_sample": nrm((DEC_BATCH, d), 1.0),
        "state_lru_h": nrm((N_A_LAYERS, DEC_BATCH, w), 0.5),
        "state_lru_conv": nrm((N_A_LAYERS, DEC_BATCH, LRU_CONV_WIDTH - 1, w), 1.0),
        "state_dwconv": nrm((N_B_LAYERS, DEC_BATCH, CF_CONV_WIDTH - 1, d), 1.0),
        "ada_w": nrm((DEPTH, d, 6 * d), 0.5 * d ** -0.5),
        "ada_b": nrm((DEPTH, 6 * d), 0.01),
        "norm_mix": 1.0 + nrm((DEPTH, d), 0.02),
        "norm_ffn": 1.0 + nrm((DEPTH, d), 0.02),
        "norm_final": 1.0 + nrm((d,), 0.02),
        "lru_w_in": nrm((N_A_LAYERS, d, 2 * w), d ** -0.5),
        "lru_b_in": nrm((N_A_LAYERS, 2 * w), 0.01),
        "lru_conv_w": nrm((N_A_LAYERS, LRU_CONV_WIDTH, w), LRU_CONV_WIDTH ** -0.5),
        "lru_conv_b": nrm((N_A_LAYERS, w), 0.01),
        "lru_gate_a_w": nrm((N_A_LAYERS, LRU_HEADS, LRU_BLOCK, LRU_BLOCK), LRU_BLOCK ** -0.5),
        "lru_gate_a_b": nrm((N_A_LAYERS, w), 0.01),
        "lru_gate_x_w": nrm((N_A_LAYERS, LRU_HEADS, LRU_BLOCK, LRU_BLOCK), LRU_BLOCK ** -0.5),
        "lru_gate_x_b": nrm((N_A_LAYERS, w), 0.01),
        "lru_lambda": lam,
        "lru_w_out": nrm((N_A_LAYERS, w, d), w ** -0.5),
        "lru_b_out": nrm((N_A_LAYERS, d), 0.01),
        "cf_w_pw1": nrm((N_B_LAYERS, d, 2 * d), d ** -0.5),
        "cf_b_pw1": nrm((N_B_LAYERS, 2 * d), 0.01),
        "cf_dw_w": nrm((N_B_LAYERS, CF_CONV_WIDTH, d), CF_CONV_WIDTH ** -0.5),
        "cf_dw_b": nrm((N_B_LAYERS, d), 0.01),
        "cf_ln_g": 1.0 + nrm((N_B_LAYERS, d), 0.02),
        "cf_ln_b": nrm((N_B_LAYERS, d), 0.01),
        "cf_w_pw2": nrm((N_B_LAYERS, d, d), d ** -0.5),
        "cf_b_pw2": nrm((N_B_LAYERS, d), 0.01),
        "peer_w_q": nrm((DEPTH, d, PEER_HEADS * PEER_DK), d ** -0.5),
        "peer_k1": nrm((DEPTH, PEER_NKEYS, PEER_DK // 2), (PEER_DK // 2) ** -0.5),
        "peer_k2": nrm((DEPTH, PEER_NKEYS, PEER_DK // 2), (PEER_DK // 2) ** -0.5),
        "peer_u": nrm((DEPTH, PEER_NEXPERTS, d), d ** -0.5),
        "peer_v": nrm((DEPTH, PEER_NEXPERTS, d), PEER_HEADS ** -0.5),
    }


def reference(x_prompt, x_sample, c_prompt, c_sample, state_lru_h, state_lru_conv, state_dwconv,
              ada_w, ada_b, norm_mix, norm_ffn, norm_final,
              lru_w_in, lru_b_in, lru_conv_w, lru_conv_b, lru_gate_a_w, lru_gate_a_b,
              lru_gate_x_w, lru_gate_x_b, lru_lambda, lru_w_out, lru_b_out,
              cf_w_pw1, cf_b_pw1, cf_dw_w, cf_dw_b, cf_ln_g, cf_ln_b, cf_w_pw2, cf_b_pw2,
              peer_w_q, peer_k1, peer_k2, peer_u, peer_v):
    p = dict(ada_w=ada_w, ada_b=ada_b, norm_mix=norm_mix, norm_ffn=norm_ffn, norm_final=norm_final,
             lru_w_in=lru_w_in, lru_b_in=lru_b_in, lru_conv_w=lru_conv_w, lru_conv_b=lru_conv_b,
             lru_gate_a_w=lru_gate_a_w, lru_gate_a_b=lru_gate_a_b, lru_gate_x_w=lru_gate_x_w,
             lru_gate_x_b=lru_gate_x_b, lru_lambda=lru_lambda, lru_w_out=lru_w_out, lru_b_out=lru_b_out,
             cf_w_pw1=cf_w_pw1, cf_b_pw1=cf_b_pw1, cf_dw_w=cf_dw_w, cf_dw_b=cf_dw_b, cf_ln_g=cf_ln_g,
             cf_ln_b=cf_ln_b, cf_w_pw2=cf_w_pw2, cf_b_pw2=cf_b_pw2, peer_w_q=peer_w_q, peer_k1=peer_k1,
             peer_k2=peer_k2, peer_u=peer_u, peer_v=peer_v)
    bp = x_prompt.shape[0]
    dt = x_prompt.dtype
    h0 = jnp.zeros((N_A_LAYERS, bp, LRU_WIDTH), dt)
    conv0 = jnp.zeros((N_A_LAYERS, bp, LRU_CONV_WIDTH - 1, LRU_WIDTH), dt)
    dw0 = jnp.zeros((N_B_LAYERS, bp, CF_CONV_WIDTH - 1, D_MODEL), dt)
    y_prompt, lru_h_p, lru_conv_p, dwconv_p = trunk(x_prompt, c_prompt, h0, conv0, dw0, p)
    y_sample, lru_h_s, lru_conv_s, dwconv_s = trunk(x_sample, c_sample, state_lru_h, state_lru_conv,
                                                    state_dwconv, p)
    return (y_prompt, y_sample, lru_h_p, lru_conv_p, dwconv_p, lru_h_s, lru_conv_s, dwconv_s)
```

```python
import functools

import jax
import jax.numpy as jnp
from jax import lax
from jax.experimental import pallas as pl
from jax.experimental.pallas import tpu as pltpu

F32 = jnp.float32
BF16 = jnp.bfloat16
I32 = jnp.int32

D_MODEL = 1024
LRU_WIDTH = 1024
LRU_HEADS = 8
LRU_BLOCK = LRU_WIDTH // LRU_HEADS
LRU_CONV_WIDTH = 4
LRU_C = 8.0
CF_CONV_WIDTH = 31
PEER_HEADS = 8
PEER_NKEYS = 128
PEER_NEXPERTS = PEER_NKEYS * PEER_NKEYS
PEER_DK = 256
PEER_TOPK = 16
PEER_PICKS = PEER_HEADS * PEER_TOPK
EPS = 1e-6

LANES = 128
SUBLANES = 8
HALF_WORDS = D_MODEL // 2
HALF_SUB = HALF_WORDS // LANES
VMEM_TABLE_LIMIT = 48 * 1024 * 1024
VMEM_MIXER_LIMIT = 56 * 1024 * 1024


def _rms_mod(x, gn, sh, sc):
    ms = jnp.mean(x * x, axis=-1, keepdims=True)
    y = x * lax.rsqrt(ms + EPS) * gn
    return y * (1.0 + sc) + sh


def _split_bf16(a):
    hi = a.astype(BF16)
    lo = (a - hi.astype(F32)).astype(BF16)
    return hi, lo


def _dot(a, b):
    return jnp.dot(a, b, preferred_element_type=F32)


def _ada_kernel(c_ref, w_ref, b_ref, o_ref):
    c = c_ref[...]
    a = c * jax.nn.sigmoid(c)
    a_hi, a_lo = _split_bf16(a)
    w_hi, w_lo = _split_bf16(w_ref[0])
    o_ref[0] = _dot(a_hi, w_hi) + _dot(a_lo, w_hi) + _dot(a_hi, w_lo) + b_ref[0]


def _ada(c_all, ada_w, ada_b):
    nl, d, n6 = ada_w.shape
    nb = c_all.shape[0]
    tn = 1024
    return pl.pallas_call(
        _ada_kernel,
        grid=(nl, n6 // tn),
        in_specs=[
            pl.BlockSpec((nb, d), lambda l, j: (0, 0)),
            pl.BlockSpec((1, d, tn), lambda l, j: (l, 0, j)),
            pl.BlockSpec((1, 1, tn), lambda l, j: (l, 0, j)),
        ],
        out_specs=pl.BlockSpec((1, nb, tn), lambda l, j: (l, 0, j)),
        out_shape=jax.ShapeDtypeStruct((nl, nb, n6), F32),
        name="ada_mod",
    )(c_all, ada_w, ada_b.reshape(nl, 1, n6))


def _lin_scan(a, b, tt):
    row = lax.broadcasted_iota(I32, a.shape, 0)
    s = 1
    while s < tt:
        keep = row >= s
        a_sh = jnp.where(keep, pltpu.roll(a, s, 0), 1.0)
        b_sh = jnp.where(keep, pltpu.roll(b, s, 0), 0.0)
        b = a * b_sh + b
        a = a * a_sh
        s *= 2
    return a, b


def _lru_kernel(tt, x_ref, mod_ref, nmix_ref, nffn_ref, win_ref, bin_ref, cw_ref, cb_ref,
                gaw_ref, gab_ref, gxw_ref, gxb_ref, lam_ref, wout_ref, bout_ref, h0_ref, c0_ref,
                x1_ref, h2_ref, newh_ref, newc_ref, hcar, win):
    j = pl.program_id(1)
    nj = pl.num_programs(1)
    hist = SUBLANES - (LRU_CONV_WIDTH - 1)

    @pl.when(j == 0)
    def _():
        hcar[...] = h0_ref[0]
        win[hist:SUBLANES, :] = c0_ref[0]

    x = x_ref[0]
    sh1, sc1, g1 = mod_ref[0, 0:1, :], mod_ref[0, 1:2, :], mod_ref[0, 2:3, :]
    sh2, sc2 = mod_ref[0, 3:4, :], mod_ref[0, 4:5, :]
    h = _rms_mod(x, nmix_ref[...], sh1, sc1)
    proj = _dot(h.astype(BF16), win_ref[...]) + bin_ref[...]
    gate = jax.nn.gelu(proj[:, :LRU_WIDTH])
    win[SUBLANES:SUBLANES + tt, :] = proj[:, LRU_WIDTH:]

    xc = cb_ref[...] + win[pl.ds(hist, tt), :] * cw_ref[0:1, :]
    for k in range(1, LRU_CONV_WIDTH):
        xc = xc + win[pl.ds(hist + k, tt), :] * cw_ref[k:k + 1, :]
    newc = win[tt + hist:tt + SUBLANES, :]
    win[hist:SUBLANES, :] = newc

    xcb = xc.astype(BF16)
    ga = jnp.concatenate(
        [_dot(xcb[:, i * LRU_BLOCK:(i + 1) * LRU_BLOCK], gaw_ref[i]) for i in range(LRU_HEADS)], axis=-1)
    gx = jnp.concatenate(
        [_dot(xcb[:, i * LRU_BLOCK:(i + 1) * LRU_BLOCK], gxw_ref[i]) for i in range(LRU_HEADS)], axis=-1)
    r = jax.nn.sigmoid(ga + gab_ref[...])
    gi = jax.nn.sigmoid(gx + gxb_ref[...])
    nl = -lam_ref[...]
    softplus = jnp.maximum(nl, 0.0) + jnp.log(1.0 + jnp.exp(-jnp.abs(nl)))
    log_a = (-LRU_C) * r * softplus
    a = jnp.exp(log_a)
    mult = jnp.sqrt(1.0 - jnp.exp(2.0 * log_a))
    bv = mult * (gi * xc)
    a_cum, hs0 = _lin_scan(a, bv, tt)
    hs = hs0 + a_cum * hcar[...]
    hcar[...] = hs[tt - 1:tt, :]

    out = _dot((hs * gate).astype(BF16), wout_ref[...]) + bout_ref[...]
    x1 = x + g1 * out
    x1_ref[0] = x1
    h2_ref[0] = _rms_mod(x1, nffn_ref[...], sh2, sc2)

    @pl.when(j == nj - 1)
    def _():
        newh_ref[0] = hcar[...]
        newc_ref[0] = newc


def _lru_layer(x, mod, nmix, nffn, p, h0, c0, tt):
    bsz, t, d = x.shape
    w = LRU_WIDTH
    cw = LRU_CONV_WIDTH
    row = lambda b, j: (b, j, 0)
    per_b = lambda b, j: (b, 0, 0)
    c2 = lambda b, j: (0, 0)
    c3 = lambda b, j: (0, 0, 0)
    return pl.pallas_call(
        functools.partial(_lru_kernel, tt),
        grid=(bsz, t // tt),
        in_specs=[
            pl.BlockSpec((1, tt, d), row),
            pl.BlockSpec((1, 6, d), per_b),
            pl.BlockSpec((1, d), c2), pl.BlockSpec((1, d), c2),
            pl.BlockSpec((d, 2 * w), c2), pl.BlockSpec((1, 2 * w), c2),
            pl.BlockSpec((cw, w), c2), pl.BlockSpec((1, w), c2),
            pl.BlockSpec((LRU_HEADS, LRU_BLOCK, LRU_BLOCK), c3), pl.BlockSpec((1, w), c2),
            pl.BlockSpec((LRU_HEADS, LRU_BLOCK, LRU_BLOCK), c3), pl.BlockSpec((1, w), c2),
            pl.BlockSpec((1, w), c2),
            pl.BlockSpec((w, d), c2), pl.BlockSpec((1, d), c2),
            pl.BlockSpec((1, 1, w), per_b), pl.BlockSpec((1, cw - 1, w), per_b),
        ],
        out_specs=[
            pl.BlockSpec((1, tt, d), row), pl.BlockSpec((1, tt, d), row),
            pl.BlockSpec((1, 1, w), per_b), pl.BlockSpec((1, cw - 1, w), per_b),
        ],
        out_shape=[
            jax.ShapeDtypeStruct((bsz, t, d), F32), jax.ShapeDtypeStruct((bsz, t, d), F32),
            jax.ShapeDtypeStruct((bsz, 1, w), F32), jax.ShapeDtypeStruct((bsz, cw - 1, w), F32),
        ],
        scratch_shapes=[pltpu.VMEM((1, w), F32), pltpu.VMEM((SUBLANES + tt, w), F32)],
        compiler_params=pltpu.CompilerParams(
            dimension_semantics=("arbitrary", "arbitrary"), vmem_limit_bytes=VMEM_MIXER_LIMIT),
        name="rglru_mixer",
    )(x, mod, nmix, nffn, p["w_in"], p["b_in"], p["conv_w"], p["conv_b"], p["ga_w"], p["ga_b"],
      p["gx_w"], p["gx_b"], p["lam"], p["w_out"], p["b_out"], h0, c0)


CF_HIST = 32


def _conf_kernel(tt, x_ref, mod_ref, nmix_ref, nffn_ref, w1_ref, b1_ref, dw_ref, db_ref,
                 lng_ref, lnb_ref, w2_ref, b2_ref, d0_ref, x1_ref, h2_ref, newd_ref, win):
    j = pl.program_id(1)
    nj = pl.num_programs(1)
    nh = CF_CONV_WIDTH - 1
    first = CF_HIST - nh

    @pl.when(j == 0)
    def _():
        win[first:CF_HIST, :] = d0_ref[0]

    x = x_ref[0]
    sh1, sc1, g1 = mod_ref[0, 0:1, :], mod_ref[0, 1:2, :], mod_ref[0, 2:3, :]
    sh2, sc2 = mod_ref[0, 3:4, :], mod_ref[0, 4:5, :]
    h = _rms_mod(x, nmix_ref[...], sh1, sc1)
    proj = _dot(h.astype(BF16), w1_ref[...]) + b1_ref[...]
    glu = proj[:, :D_MODEL] * jax.nn.sigmoid(proj[:, D_MODEL:])
    win[CF_HIST:CF_HIST + tt, :] = glu

    dcv = db_ref[...] + win[pl.ds(first, tt), :] * dw_ref[0:1, :]
    for k in range(1, CF_CONV_WIDTH):
        dcv = dcv + win[pl.ds(first + k, tt), :] * dw_ref[k:k + 1, :]
    newd = win[tt + first:tt + CF_HIST, :]
    win[first:CF_HIST, :] = newd

    mu = jnp.mean(dcv, axis=-1, keepdims=True)
    cen = dcv - mu
    var = jnp.mean(cen * cen, axis=-1, keepdims=True)
    ln = cen * lax.rsqrt(var + EPS) * lng_ref[...] + lnb_ref[...]
    act = ln * jax.nn.sigmoid(ln)
    out = _dot(act.astype(BF16), w2_ref[...]) + b2_ref[...]
    x1 = x + g1 * out
    x1_ref[0] = x1
    h2_ref[0] = _rms_mod(x1, nffn_ref[...], sh2, sc2)

    @pl.when(j == nj - 1)
    def _():
        newd_ref[0] = newd


def _conf_layer(x, mod, nmix, nffn, p, d0, tt):
    bsz, t, d = x.shape
    nh = CF_CONV_WIDTH - 1
    row = lambda b, j: (b, j, 0)
    per_b = lambda b, j: (b, 0, 0)
    c2 = lambda b, j: (0, 0)
    return pl.pallas_call(
        functools.partial(_conf_kernel, tt),
        grid=(bsz, t // tt),
        in_specs=[
            pl.BlockSpec((1, tt, d), row),
            pl.BlockSpec((1, 6, d), per_b),
            pl.BlockSpec((1, d), c2), pl.BlockSpec((1, d), c2),
            pl.BlockSpec((d, 2 * d), c2), pl.BlockSpec((1, 2 * d), c2),
            pl.BlockSpec((CF_CONV_WIDTH, d), c2), pl.BlockSpec((1, d), c2),
            pl.BlockSpec((1, d), c2), pl.BlockSpec((1, d), c2),
            pl.BlockSpec((d, d), c2), pl.BlockSpec((1, d), c2),
            pl.BlockSpec((1, nh, d), per_b),
        ],
        out_specs=[
            pl.BlockSpec((1, tt, d), row), pl.BlockSpec((1, tt, d), row),
            pl.BlockSpec((1, nh, d), per_b),
        ],
        out_shape=[
            jax.ShapeDtypeStruct((bsz, t, d), F32), jax.ShapeDtypeStruct((bsz, t, d), F32),
            jax.ShapeDtypeStruct((bsz, nh, d), F32),
        ],
        scratch_shapes=[pltpu.VMEM((CF_HIST + tt, d), F32)],
        compiler_params=pltpu.CompilerParams(
            dimension_semantics=("arbitrary", "arbitrary"), vmem_limit_bytes=VMEM_MIXER_LIMIT),
        name="conformer_mixer",
    )(x, mod, nmix, nffn, p["w_pw1"], p["b_pw1"], p["dw_w"], p["dw_b"], p["ln_g"], p["ln_b"],
      p["w_pw2"], p["b_pw2"], d0)


def _top16_rows(s):
    nrows = s.shape[0]
    row = lax.broadcasted_iota(I32, s.shape, 0)
    vals, ids = [], []
    for _ in range(PEER_TOPK):
        m = jnp.max(s, axis=0, keepdims=True)
        idx = jnp.min(jnp.where(s == m, row, nrows), axis=0, keepdims=True)
        vals.append(m)
        ids.append(idx)
        s = jnp.where(row == idx, -jnp.inf, s)
    return jnp.concatenate(vals, axis=0), jnp.concatenate(ids, axis=0)


def _take_rows16(table, sel):
    out = jnp.zeros_like(table)
    for a in range(PEER_TOPK):
        out = jnp.where(sel == a, table[a:a + 1, :], out)
    return out


def _peer_q_kernel(h_ref, wq_ref, k1_ref, k2_ref, e_ref, g_ref):
    half = PEER_DK // 2
    q = _dot(h_ref[...].astype(BF16), wq_ref[...])
    nt = (((1,), (1,)), ((), ()))
    s1 = lax.dot_general(k1_ref[...], q[:, :half].astype(BF16), nt, preferred_element_type=F32)
    s2 = lax.dot_general(k2_ref[...], q[:, half:].astype(BF16), nt, preferred_element_type=F32)
    s1v, s1i = _top16_rows(s1)
    s2v, s2i = _top16_rows(s2)
    cand = jnp.concatenate([s1v[a:a + 1, :] + s2v for a in range(PEER_TOPK)], axis=0)
    sv, si = _top16_rows(cand)
    idx1 = _take_rows16(s1i, si >> 4)
    idx2 = _take_rows16(s2i, si & (PEER_TOPK - 1))
    e_ref[0] = idx1 * PEER_NKEYS + idx2
    ex = jnp.exp(sv - sv[0:1, :])
    g_ref[0] = ex / jnp.sum(ex, axis=0, keepdims=True)


def _peer_q(h2, wq, k1, k2, tq):
    ntok, d = h2.shape
    return pl.pallas_call(
        _peer_q_kernel,
        grid=(ntok // tq, PEER_HEADS),
        in_specs=[
            pl.BlockSpec((tq, d), lambda i, h: (i, 0)),
            pl.BlockSpec((d, PEER_DK), lambda i, h: (0, h)),
            pl.BlockSpec((PEER_NKEYS, PEER_DK // 2), lambda i, h: (0, 0)),
            pl.BlockSpec((PEER_NKEYS, PEER_DK // 2), lambda i, h: (0, 0)),
        ],
        out_specs=[
            pl.BlockSpec((1, PEER_TOPK, tq), lambda i, h: (h, 0, i)),
            pl.BlockSpec((1, PEER_TOPK, tq), lambda i, h: (h, 0, i)),
        ],
        out_shape=[
            jax.ShapeDtypeStruct((PEER_HEADS, PEER_TOPK, ntok), I32),
            jax.ShapeDtypeStruct((PEER_HEADS, PEER_TOPK, ntok), F32),
        ],
        compiler_params=pltpu.CompilerParams(dimension_semantics=("arbitrary", "arbitrary")),
        name="peer_query_topk",
    )(h2, wq, k1, k2)


def _pack_table(tab):
    lo = lax.bitcast_convert_type(tab[:, :HALF_WORDS].astype(BF16), jnp.uint16).astype(jnp.uint32)
    hi = lax.bitcast_convert_type(tab[:, HALF_WORDS:].astype(BF16), jnp.uint16).astype(jnp.uint32)
    words = lax.bitcast_convert_type(lo | (hi << 16), I32)
    return words.reshape(tab.shape[0], HALF_SUB, LANES)


def _unpack_pair(tab_ref, a, b):
    x = jnp.concatenate([tab_ref[a], tab_ref[b]], axis=0)
    lo = pltpu.bitcast(x << 16, F32)
    hi = pltpu.bitcast(x & jnp.int32(-65536), F32)
    return lo, hi


def _peer_u_kernel(tt, e_ref, h_ref, g_ref, tab_ref, w_ref):
    sub = lax.broadcasted_iota(I32, (SUBLANES, LANES), 0)
    low = sub < HALF_SUB
    sub4 = sub & (HALF_SUB - 1)
    ones = jnp.ones((SUBLANES, LANES), BF16)
    nt = (((1,), (1,)), ((), ()))

    def body(t, carry):
        v = h_ref[t]
        vr = pltpu.roll(v, HALF_SUB, 0)
        h_lo = jnp.where(low, v, vr)
        h_hi = jnp.where(low, vr, v)
        slabs = []
        for grp in range(PEER_PICKS // SUBLANES):
            es = []
            for i in range(HALF_SUB):
                a = e_ref[t, grp * SUBLANES + i]
                b = e_ref[t, grp * SUBLANES + HALF_SUB + i]
                lo, hi = _unpack_pair(tab_ref, a, b)
                p = lo * h_lo + hi * h_hi
                p = p + pltpu.roll(p, SUBLANES - 2, 0)
                p = p + pltpu.roll(p, SUBLANES - 1, 0)
                es.append(p if i == 0 else pltpu.roll(p, i, 0))
            f = es[HALF_SUB - 1]
            for i in range(HALF_SUB - 2, -1, -1):
                f = jnp.where(sub4 == i, es[i], f)
            slabs.append(f)
        part = jnp.concatenate(slabs, axis=0)
        p_hi, p_lo = _split_bf16(part)
        z = (lax.dot_general(ones, p_hi, nt, preferred_element_type=F32)
             + lax.dot_general(ones, p_lo, nt, preferred_element_type=F32))[0:1, :]
        w_ref[t] = g_ref[t] * jax.nn.gelu(z)
        return carry

    lax.fori_loop(0, tt, body, 0)


def _peer_u(e, h2, g, tab, tt):
    ntok = e.shape[0]
    return pl.pallas_call(
        functools.partial(_peer_u_kernel, tt),
        grid=(ntok // tt,),
        in_specs=[
            pl.BlockSpec((tt, PEER_PICKS), lambda i: (i, 0), memory_space=pltpu.SMEM),
            pl.BlockSpec((tt, SUBLANES, LANES), lambda i: (i, 0, 0)),
            pl.BlockSpec((tt, 1, PEER_PICKS), lambda i: (i, 0, 0)),
            pl.BlockSpec((PEER_NEXPERTS, HALF_SUB, LANES), lambda i: (0, 0, 0),
                         pipeline_mode=pl.Buffered(1)),
        ],
        out_specs=pl.BlockSpec((tt, 1, PEER_PICKS), lambda i: (i, 0, 0)),
        out_shape=jax.ShapeDtypeStruct((ntok, 1, PEER_PICKS), F32),
        compiler_params=pltpu.CompilerParams(
            dimension_semantics=("arbitrary",), vmem_limit_bytes=VMEM_TABLE_LIMIT),
        name="peer_down_gather",
    )(e, h2.reshape(ntok, SUBLANES, LANES), g.reshape(ntok, 1, PEER_PICKS), tab)


def _peer_v_kernel(tt, final, e_ref, w_ref, x_ref, g2_ref, nf_ref, tab_ref, o_ref):
    sub = lax.broadcasted_iota(I32, (SUBLANES, LANES), 0)
    low = sub < HALF_SUB
    nacc = 2

    def body(t, carry):
        acc_lo = [jnp.zeros((SUBLANES, LANES), F32) for _ in range(nacc)]
        acc_hi = [jnp.zeros((SUBLANES, LANES), F32) for _ in range(nacc)]
        for pr in range(PEER_PICKS // 2):
            a = e_ref[t, 2 * pr]
            b = e_ref[t, 2 * pr + 1]
            wv = jnp.where(low, w_ref[t, 2 * pr], w_ref[t, 2 * pr + 1])
            lo, hi = _unpack_pair(tab_ref, a, b)
            acc_lo[pr % nacc] = acc_lo[pr % nacc] + lo * wv
            acc_hi[pr % nacc] = acc_hi[pr % nacc] + hi * wv
        s_lo = acc_lo[0] + acc_lo[1]
        s_hi = acc_hi[0] + acc_hi[1]
        s_lo = s_lo + pltpu.roll(s_lo, HALF_SUB, 0)
        s_hi = s_hi + pltpu.roll(s_hi, HALF_SUB, 0)
        ffn = jnp.where(low, s_lo, s_hi)
        x2 = x_ref[t] + g2_ref[0] * ffn
        if final:
            ms = jnp.sum(x2 * x2, keepdims=True) * (1.0 / D_MODEL)
            x2 = x2 * lax.rsqrt(ms + EPS) * nf_ref[...]
        o_ref[t] = x2
        return carry

    lax.fori_loop(0, tt, body, 0)


def _peer_v(e, w, x1, g2, nfinal, tab, tt, final):
    bsz, t, d = x1.shape
    ntok = bsz * t
    per_seq = t // tt
    return pl.pallas_call(
        functools.partial(_peer_v_kernel, tt, final),
        grid=(ntok // tt,),
        in_specs=[
            pl.BlockSpec((tt, PEER_PICKS), lambda i: (i, 0), memory_space=pltpu.SMEM),
            pl.BlockSpec((tt, PEER_PICKS), lambda i: (i, 0), memory_space=pltpu.SMEM),
            pl.BlockSpec((tt, SUBLANES, LANES), lambda i: (i, 0, 0)),
            pl.BlockSpec((1, SUBLANES, LANES), lambda i: (i // per_seq, 0, 0)),
            pl.BlockSpec((SUBLANES, LANES), lambda i: (0, 0)),
            pl.BlockSpec((PEER_NEXPERTS, HALF_SUB, LANES), lambda i: (0, 0, 0),
                         pipeline_mode=pl.Buffered(1)),
        ],
        out_specs=pl.BlockSpec((tt, SUBLANES, LANES), lambda i: (i, 0, 0)),
        out_shape=jax.ShapeDtypeStruct((ntok, SUBLANES, LANES), F32),
        compiler_params=pltpu.CompilerParams(
            dimension_semantics=("arbitrary",), vmem_limit_bytes=VMEM_TABLE_LIMIT),
        name="peer_up_gather",
    )(e, w, x1.reshape(ntok, SUBLANES, LANES), g2.reshape(bsz, SUBLANES, LANES),
      nfinal.reshape(SUBLANES, LANES), tab).reshape(bsz, t, d)


def _peer_ffn(x1, h2, g2, nfinal, pp, final):
    bsz, t, d = x1.shape
    ntok = bsz * t
    tq = 256
    et, gt = _peer_q(h2.reshape(ntok, d), pp["w_q"], pp["k1"], pp["k2"], tq)
    e = et.transpose(2, 0, 1).reshape(ntok, PEER_PICKS)
    g = gt.transpose(2, 0, 1).reshape(ntok, PEER_PICKS)
    w = _peer_u(e, h2.reshape(ntok, d), g, pp["u"], min(ntok, 128))
    return _peer_v(e, w.reshape(ntok, PEER_PICKS), x1, g2, nfinal, pp["v"], min(t, 128), final)


def _trunk(x, mod, lru_h, lru_conv, dwconv, prm):
    bsz, t, _ = x.shape
    tt = min(t, 256)
    m0, m1 = mod[0], mod[1]
    x1, h2, new_h, new_conv = _lru_layer(x, m0, prm["norm_mix"][0], prm["norm_ffn"][0], prm["lru"],
                                         lru_h, lru_conv, tt)
    x2 = _peer_ffn(x1, h2, m0[:, 5], prm["norm_final"], prm["peer"][0], False)
    x3, h4, new_dw = _conf_layer(x2, m1, prm["norm_mix"][1], prm["norm_ffn"][1], prm["cf"], dwconv, tt)
    y = _peer_ffn(x3, h4, m1[:, 5], prm["norm_final"], prm["peer"][1], True)
    return y, new_h, new_conv, new_dw


def kernel(x_prompt, x_sample, c_prompt, c_sample, state_lru_h, state_lru_conv, state_dwconv, ada_w, ada_b, norm_mix, norm_ffn, norm_final, lru_w_in, lru_b_in, lru_conv_w, lru_conv_b, lru_gate_a_w, lru_gate_a_b, lru_gate_x_w, lru_gate_x_b, lru_lambda, lru_w_out, lru_b_out, cf_w_pw1, cf_b_pw1, cf_dw_w, cf_dw_b, cf_ln_g, cf_ln_b, cf_w_pw2, cf_b_pw2, peer_w_q, peer_k1, peer_k2, peer_u, peer_v):
    bp, bs = x_prompt.shape[0], x_sample.shape[0]
    depth = ada_w.shape[0]
    d = D_MODEL
    row = lambda a: a.reshape(1, -1)
    prm = dict(
        norm_mix=[row(norm_mix[l]) for l in range(depth)],
        norm_ffn=[row(norm_ffn[l]) for l in range(depth)],
        norm_final=norm_final,
        lru=dict(w_in=lru_w_in[0].astype(BF16), b_in=row(lru_b_in[0]), conv_w=lru_conv_w[0],
                 conv_b=row(lru_conv_b[0]), ga_w=lru_gate_a_w[0].astype(BF16), ga_b=row(lru_gate_a_b[0]),
                 gx_w=lru_gate_x_w[0].astype(BF16), gx_b=row(lru_gate_x_b[0]), lam=row(lru_lambda[0]),
                 w_out=lru_w_out[0].astype(BF16), b_out=row(lru_b_out[0])),
        cf=dict(w_pw1=cf_w_pw1[0].astype(BF16), b_pw1=row(cf_b_pw1[0]), dw_w=cf_dw_w[0],
                dw_b=row(cf_dw_b[0]), ln_g=row(cf_ln_g[0]), ln_b=row(cf_ln_b[0]),
                w_pw2=cf_w_pw2[0].astype(BF16), b_pw2=row(cf_b_pw2[0])),
        peer=[dict(w_q=peer_w_q[l].astype(BF16), k1=peer_k1[l].astype(BF16), k2=peer_k2[l].astype(BF16),
                   u=_pack_table(peer_u[l]), v=_pack_table(peer_v[l])) for l in range(depth)],
    )
    mod = _ada(jnp.concatenate([c_prompt, c_sample], axis=0), ada_w, ada_b)
    mod = mod.reshape(depth, bp + bs, 6, d)
    dt = x_prompt.dtype
    zero_h = jnp.zeros((bp, 1, LRU_WIDTH), dt)
    zero_c = jnp.zeros((bp, LRU_CONV_WIDTH - 1, LRU_WIDTH), dt)
    zero_d = jnp.zeros((bp, CF_CONV_WIDTH - 1, d), dt)
    y_p, h_p, c_p, d_p = _trunk(x_prompt, mod[:, :bp], zero_h, zero_c, zero_d, prm)
    y_s, h_s, c_s, d_s = _trunk(x_sample, mod[:, bp:], state_lru_h[0][:, None, :], state_lru_conv[0],
                                state_dwconv[0], prm)
    return (y_p, y_s, h_p.reshape(1, bp, LRU_WIDTH), c_p[None], d_p[None],
            h_s.reshape(1, bs, LRU_WIDTH), c_s[None], d_s[None])
```

```python
import functools

import jax
import jax.numpy as jnp
from jax import lax
from jax.experimental import pallas as pl
from jax.experimental.pallas import tpu as pltpu

F32 = jnp.float32
BF16 = jnp.bfloat16
I32 = jnp.int32

D_MODEL = 1024
LRU_WIDTH = 1024
LRU_HEADS = 8
LRU_BLOCK = LRU_WIDTH // LRU_HEADS
LRU_CONV_WIDTH = 4
LRU_C = 8.0
CF_CONV_WIDTH = 31
PEER_HEADS = 8
PEER_NKEYS = 128
PEER_NEXPERTS = PEER_NKEYS * PEER_NKEYS
PEER_DK = 256
PEER_TOPK = 16
PEER_PICKS = PEER_HEADS * PEER_TOPK
EPS = 1e-6

LANES = 128
SUBLANES = 8
HALF_WORDS = D_MODEL // 2
HALF_SUB = HALF_WORDS // LANES
STAGE_ROWS = PEER_PICKS * 2 * HALF_SUB
GROUP = 8
VMEM_TABLE_LIMIT = 48 * 1024 * 1024
VMEM_MIXER_LIMIT = 56 * 1024 * 1024


def _rms_mod(x, gn, sh, sc):
    ms = jnp.mean(x * x, axis=-1, keepdims=True)
    y = x * lax.rsqrt(ms + EPS) * gn
    return y * (1.0 + sc) + sh


def _split_bf16(a):
    hi = a.astype(BF16)
    lo = (a - hi.astype(F32)).astype(BF16)
    return hi, lo


def _dot(a, b):
    return jnp.dot(a, b, preferred_element_type=F32)


def _ada_kernel(c_ref, w_ref, b_ref, o_ref):
    c = c_ref[...]
    a = c * jax.nn.sigmoid(c)
    a_hi, a_lo = _split_bf16(a)
    w_hi, w_lo = _split_bf16(w_ref[0])
    o_ref[0] = _dot(a_hi, w_hi) + _dot(a_lo, w_hi) + _dot(a_hi, w_lo) + b_ref[0]


def _ada(c_all, ada_w, ada_b):
    nl, d, n6 = ada_w.shape
    nb = c_all.shape[0]
    tn = 1024
    return pl.pallas_call(
        _ada_kernel,
        grid=(nl, n6 // tn),
        in_specs=[
            pl.BlockSpec((nb, d), lambda l, j: (0, 0)),
            pl.BlockSpec((1, d, tn), lambda l, j: (l, 0, j)),
            pl.BlockSpec((1, 1, tn), lambda l, j: (l, 0, j)),
        ],
        out_specs=pl.BlockSpec((1, nb, tn), lambda l, j: (l, 0, j)),
        out_shape=jax.ShapeDtypeStruct((nl, nb, n6), F32),
        name="ada_mod",
    )(c_all, ada_w, ada_b.reshape(nl, 1, n6))


def _lin_scan(a, b, tt):
    row = lax.broadcasted_iota(I32, a.shape, 0)
    s = 1
    while s < tt:
        keep = row >= s
        a_sh = jnp.where(keep, pltpu.roll(a, s, 0), 1.0)
        b_sh = jnp.where(keep, pltpu.roll(b, s, 0), 0.0)
        b = a * b_sh + b
        a = a * a_sh
        s *= 2
    return a, b


def _lru_kernel(tt, x_ref, mod_ref, nmix_ref, nffn_ref, win_ref, bin_ref, cw_ref, cb_ref,
                gaw_ref, gab_ref, gxw_ref, gxb_ref, lam_ref, wout_ref, bout_ref, h0_ref, c0_ref,
                x1_ref, h2_ref, newh_ref, newc_ref, hcar, win):
    j = pl.program_id(1)
    nj = pl.num_programs(1)
    hist = SUBLANES - (LRU_CONV_WIDTH - 1)

    @pl.when(j == 0)
    def _():
        hcar[...] = h0_ref[0]
        win[hist:SUBLANES, :] = c0_ref[0]

    x = x_ref[0]
    sh1, sc1, g1 = mod_ref[0, 0:1, :], mod_ref[0, 1:2, :], mod_ref[0, 2:3, :]
    sh2, sc2 = mod_ref[0, 3:4, :], mod_ref[0, 4:5, :]
    h = _rms_mod(x, nmix_ref[...], sh1, sc1)
    proj = _dot(h.astype(BF16), win_ref[...]) + bin_ref[...]
    gate = jax.nn.gelu(proj[:, :LRU_WIDTH])
    win[SUBLANES:SUBLANES + tt, :] = proj[:, LRU_WIDTH:]

    xc = cb_ref[...] + win[pl.ds(hist, tt), :] * cw_ref[0:1, :]
    for k in range(1, LRU_CONV_WIDTH):
        xc = xc + win[pl.ds(hist + k, tt), :] * cw_ref[k:k + 1, :]
    newc = win[tt + hist:tt + SUBLANES, :]
    win[hist:SUBLANES, :] = newc

    xcb = xc.astype(BF16)
    ga = jnp.concatenate(
        [_dot(xcb[:, i * LRU_BLOCK:(i + 1) * LRU_BLOCK], gaw_ref[i]) for i in range(LRU_HEADS)], axis=-1)
    gx = jnp.concatenate(
        [_dot(xcb[:, i * LRU_BLOCK:(i + 1) * LRU_BLOCK], gxw_ref[i]) for i in range(LRU_HEADS)], axis=-1)
    r = jax.nn.sigmoid(ga + gab_ref[...])
    gi = jax.nn.sigmoid(gx + gxb_ref[...])
    nl = -lam_ref[...]
    softplus = jnp.maximum(nl, 0.0) + jnp.log(1.0 + jnp.exp(-jnp.abs(nl)))
    log_a = (-LRU_C) * r * softplus
    a = jnp.exp(log_a)
    mult = jnp.sqrt(1.0 - jnp.exp(2.0 * log_a))
    bv = mult * (gi * xc)
    a_cum, hs0 = _lin_scan(a, bv, tt)
    hs = hs0 + a_cum * hcar[...]
    hcar[...] = hs[tt - 1:tt, :]

    out = _dot((hs * gate).astype(BF16), wout_ref[...]) + bout_ref[...]
    x1 = x + g1 * out
    x1_ref[0] = x1
    h2_ref[0] = _rms_mod(x1, nffn_ref[...], sh2, sc2)

    @pl.when(j == nj - 1)
    def _():
        newh_ref[0] = hcar[...]
        newc_ref[0] = newc


def _lru_layer(x, mod, nmix, nffn, p, h0, c0, tt):
    bsz, t, d = x.shape
    w = LRU_WIDTH
    cw = LRU_CONV_WIDTH
    row = lambda b, j: (b, j, 0)
    per_b = lambda b, j: (b, 0, 0)
    c2 = lambda b, j: (0, 0)
    c3 = lambda b, j: (0, 0, 0)
    return pl.pallas_call(
        functools.partial(_lru_kernel, tt),
        grid=(bsz, t // tt),
        in_specs=[
            pl.BlockSpec((1, tt, d), row),
            pl.BlockSpec((1, 6, d), per_b),
            pl.BlockSpec((1, d), c2), pl.BlockSpec((1, d), c2),
            pl.BlockSpec((d, 2 * w), c2), pl.BlockSpec((1, 2 * w), c2),
            pl.BlockSpec((cw, w), c2), pl.BlockSpec((1, w), c2),
            pl.BlockSpec((LRU_HEADS, LRU_BLOCK, LRU_BLOCK), c3), pl.BlockSpec((1, w), c2),
            pl.BlockSpec((LRU_HEADS, LRU_BLOCK, LRU_BLOCK), c3), pl.BlockSpec((1, w), c2),
            pl.BlockSpec((1, w), c2),
            pl.BlockSpec((w, d), c2), pl.BlockSpec((1, d), c2),
            pl.BlockSpec((1, 1, w), per_b), pl.BlockSpec((1, cw - 1, w), per_b),
        ],
        out_specs=[
            pl.BlockSpec((1, tt, d), row), pl.BlockSpec((1, tt, d), row),
            pl.BlockSpec((1, 1, w), per_b), pl.BlockSpec((1, cw - 1, w), per_b),
        ],
        out_shape=[
            jax.ShapeDtypeStruct((bsz, t, d), F32), jax.ShapeDtypeStruct((bsz, t, d), F32),
            jax.ShapeDtypeStruct((bsz, 1, w), F32), jax.ShapeDtypeStruct((bsz, cw - 1, w), F32),
        ],
        scratch_shapes=[pltpu.VMEM((1, w), F32), pltpu.VMEM((SUBLANES + tt, w), F32)],
        compiler_params=pltpu.CompilerParams(
            dimension_semantics=("arbitrary", "arbitrary"), vmem_limit_bytes=VMEM_MIXER_LIMIT),
        name="rglru_mixer",
    )(x, mod, nmix, nffn, p["w_in"], p["b_in"], p["conv_w"], p["conv_b"], p["ga_w"], p["ga_b"],
      p["gx_w"], p["gx_b"], p["lam"], p["w_out"], p["b_out"], h0, c0)


CF_HIST = 32


def _conf_kernel(tt, x_ref, mod_ref, nmix_ref, nffn_ref, w1_ref, b1_ref, dw_ref, db_ref,
                 lng_ref, lnb_ref, w2_ref, b2_ref, d0_ref, x1_ref, h2_ref, newd_ref, win):
    j = pl.program_id(1)
    nj = pl.num_programs(1)
    nh = CF_CONV_WIDTH - 1
    first = CF_HIST - nh

    @pl.when(j == 0)
    def _():
        win[first:CF_HIST, :] = d0_ref[0]

    x = x_ref[0]
    sh1, sc1, g1 = mod_ref[0, 0:1, :], mod_ref[0, 1:2, :], mod_ref[0, 2:3, :]
    sh2, sc2 = mod_ref[0, 3:4, :], mod_ref[0, 4:5, :]
    h = _rms_mod(x, nmix_ref[...], sh1, sc1)
    proj = _dot(h.astype(BF16), w1_ref[...]) + b1_ref[...]
    glu = proj[:, :D_MODEL] * jax.nn.sigmoid(proj[:, D_MODEL:])
    win[CF_HIST:CF_HIST + tt, :] = glu

    dcv = db_ref[...] + win[pl.ds(first, tt), :] * dw_ref[0:1, :]
    for k in range(1, CF_CONV_WIDTH):
        dcv = dcv + win[pl.ds(first + k, tt), :] * dw_ref[k:k + 1, :]
    newd = win[tt + first:tt + CF_HIST, :]
    win[first:CF_HIST, :] = newd

    mu = jnp.mean(dcv, axis=-1, keepdims=True)
    cen = dcv - mu
    var = jnp.mean(cen * cen, axis=-1, keepdims=True)
    ln = cen * lax.rsqrt(var + EPS) * lng_ref[...] + lnb_ref[...]
    act = ln * jax.nn.sigmoid(ln)
    out = _dot(act.astype(BF16), w2_ref[...]) + b2_ref[...]
    x1 = x + g1 * out
    x1_ref[0] = x1
    h2_ref[0] = _rms_mod(x1, nffn_ref[...], sh2, sc2)

    @pl.when(j == nj - 1)
    def _():
        newd_ref[0] = newd


def _conf_layer(x, mod, nmix, nffn, p, d0, tt):
    bsz, t, d = x.shape
    nh = CF_CONV_WIDTH - 1
    row = lambda b, j: (b, j, 0)
    per_b = lambda b, j: (b, 0, 0)
    c2 = lambda b, j: (0, 0)
    return pl.pallas_call(
        functools.partial(_conf_kernel, tt),
        grid=(bsz, t // tt),
        in_specs=[
            pl.BlockSpec((1, tt, d), row),
            pl.BlockSpec((1, 6, d), per_b),
            pl.BlockSpec((1, d), c2), pl.BlockSpec((1, d), c2),
            pl.BlockSpec((d, 2 * d), c2), pl.BlockSpec((1, 2 * d), c2),
            pl.BlockSpec((CF_CONV_WIDTH, d), c2), pl.BlockSpec((1, d), c2),
            pl.BlockSpec((1, d), c2), pl.BlockSpec((1, d), c2),
            pl.BlockSpec((d, d), c2), pl.BlockSpec((1, d), c2),
            pl.BlockSpec((1, nh, d), per_b),
        ],
        out_specs=[
            pl.BlockSpec((1, tt, d), row), pl.BlockSpec((1, tt, d), row),
            pl.BlockSpec((1, nh, d), per_b),
        ],
        out_shape=[
            jax.ShapeDtypeStruct((bsz, t, d), F32), jax.ShapeDtypeStruct((bsz, t, d), F32),
            jax.ShapeDtypeStruct((bsz, nh, d), F32),
        ],
        scratch_shapes=[pltpu.VMEM((CF_HIST + tt, d), F32)],
        compiler_params=pltpu.CompilerParams(
            dimension_semantics=("arbitrary", "arbitrary"), vmem_limit_bytes=VMEM_MIXER_LIMIT),
        name="conformer_mixer",
    )(x, mod, nmix, nffn, p["w_pw1"], p["b_pw1"], p["dw_w"], p["dw_b"], p["ln_g"], p["ln_b"],
      p["w_pw2"], p["b_pw2"], d0)


def _top16_rows(s):
    nrows = s.shape[0]
    row = lax.broadcasted_iota(I32, s.shape, 0)
    vals, ids = [], []
    for _ in range(PEER_TOPK):
        m = jnp.max(s, axis=0, keepdims=True)
        idx = jnp.min(jnp.where(s == m, row, nrows), axis=0, keepdims=True)
        vals.append(m)
        ids.append(idx)
        s = jnp.where(row == idx, -jnp.inf, s)
    return jnp.concatenate(vals, axis=0), jnp.concatenate(ids, axis=0)


def _take_rows16(table, sel):
    out = jnp.zeros_like(table)
    for a in range(PEER_TOPK):
        out = jnp.where(sel == a, table[a:a + 1, :], out)
    return out


def _peer_q_kernel(h_ref, wq_ref, k1_ref, k2_ref, e_ref, g_ref):
    half = PEER_DK // 2
    q = _dot(h_ref[...].astype(BF16), wq_ref[...])
    nt = (((1,), (1,)), ((), ()))
    s1 = lax.dot_general(k1_ref[...], q[:, :half].astype(BF16), nt, preferred_element_type=F32)
    s2 = lax.dot_general(k2_ref[...], q[:, half:].astype(BF16), nt, preferred_element_type=F32)
    s1v, s1i = _top16_rows(s1)
    s2v, s2i = _top16_rows(s2)
    cand = jnp.concatenate([s1v[a:a + 1, :] + s2v for a in range(PEER_TOPK)], axis=0)
    sv, si = _top16_rows(cand)
    idx1 = _take_rows16(s1i, si >> 4)
    idx2 = _take_rows16(s2i, si & (PEER_TOPK - 1))
    e_ref[0] = idx1 * PEER_NKEYS + idx2
    ex = jnp.exp(sv - sv[0:1, :])
    g_ref[0] = ex / jnp.sum(ex, axis=0, keepdims=True)


def _peer_q(h2, wq, k1, k2, tq):
    ntok, d = h2.shape
    return pl.pallas_call(
        _peer_q_kernel,
        grid=(ntok // tq, PEER_HEADS),
        in_specs=[
            pl.BlockSpec((tq, d), lambda i, h: (i, 0)),
            pl.BlockSpec((d, PEER_DK), lambda i, h: (0, h)),
            pl.BlockSpec((PEER_NKEYS, PEER_DK // 2), lambda i, h: (0, 0)),
            pl.BlockSpec((PEER_NKEYS, PEER_DK // 2), lambda i, h: (0, 0)),
        ],
        out_specs=[
            pl.BlockSpec((1, PEER_TOPK, tq), lambda i, h: (h, 0, i)),
            pl.BlockSpec((1, PEER_TOPK, tq), lambda i, h: (h, 0, i)),
        ],
        out_shape=[
            jax.ShapeDtypeStruct((PEER_HEADS, PEER_TOPK, ntok), I32),
            jax.ShapeDtypeStruct((PEER_HEADS, PEER_TOPK, ntok), F32),
        ],
        compiler_params=pltpu.CompilerParams(dimension_semantics=("arbitrary", "arbitrary")),
        name="peer_query_topk",
    )(h2, wq, k1, k2)


def _pack_table(tab):
    lo = lax.bitcast_convert_type(tab[:, :HALF_WORDS].astype(BF16), jnp.uint16).astype(jnp.uint32)
    hi = lax.bitcast_convert_type(tab[:, HALF_WORDS:].astype(BF16), jnp.uint16).astype(jnp.uint32)
    words = lax.bitcast_convert_type(lo | (hi << 16), I32)
    return words.reshape(tab.shape[0], HALF_SUB, LANES)


def _pair_pattern(nrows):
    probe = pltpu.bitcast(jnp.full((SUBLANES, LANES), 0x3F800000, I32), BF16)
    odd_is_hi = probe[1:2, 0:1].astype(F32) == 1.0
    m = lax.broadcasted_iota(I32, (nrows, STAGE_ROWS), 0) & (SUBLANES - 1)
    r = lax.broadcasted_iota(I32, (nrows, STAGE_ROWS), 1)
    chunk = (r >> 1) & (HALF_SUB - 1)
    par = r & 1
    half = jnp.where(odd_is_hi, par, 1 - par)
    return m == half * HALF_SUB + chunk


def _stage_rows(e_ref, t, tab_ref, stage_ref):
    for j in range(PEER_PICKS):
        stage_ref[j * HALF_SUB:(j + 1) * HALF_SUB, :] = tab_ref[e_ref[t, j]]
    return pltpu.bitcast(stage_ref[...], BF16)


def _pick_of_stage_row(shape, pick_axis):
    r = lax.broadcasted_iota(I32, shape, 1 - pick_axis)
    j = lax.broadcasted_iota(I32, shape, pick_axis)
    return jnp.where((r >> 3) == j, 1.0, 0.0).astype(BF16)


def _peer_u_kernel(tt, e_ref, h_ref, g_ref, tab_ref, w_ref, stage0, stage1, zs_ref):
    mask = _pair_pattern(2 * SUBLANES)
    nt = (((1,), (1,)), ((), ()))
    stages = (stage0, stage1)

    def body(tg, carry):
        base = pl.multiple_of(tg * GROUP, GROUP)
        rows = []
        for i in range(GROUP):
            t = base + i
            xb = _stage_rows(e_ref, t, tab_ref, stages[i % 2])
            h_hi, h_lo = _split_bf16(h_ref[t])
            h16 = jnp.concatenate([h_hi, h_lo], axis=0)
            out = lax.dot_general(h16, xb, nt, preferred_element_type=F32)
            m = jnp.where(mask, out, 0.0)
            rows.append(jnp.sum(m[:SUBLANES] + m[SUBLANES:], axis=0, keepdims=True))
        zs_ref[pl.ds(base, GROUP), :] = jnp.concatenate(rows, axis=0)
        return carry

    lax.fori_loop(0, tt // GROUP, body, 0)
    z_hi, z_lo = _split_bf16(zs_ref[...])
    fold = _pick_of_stage_row((STAGE_ROWS, PEER_PICKS), 1)
    z = _dot(z_hi, fold) + _dot(z_lo, fold)
    w_ref[...] = g_ref[...] * jax.nn.gelu(z)


def _peer_u(e, h2, g, tab, tt):
    ntok = e.shape[0]
    return pl.pallas_call(
        functools.partial(_peer_u_kernel, tt),
        grid=(ntok // tt,),
        in_specs=[
            pl.BlockSpec((tt, PEER_PICKS), lambda i: (i, 0), memory_space=pltpu.SMEM),
            pl.BlockSpec((tt, SUBLANES, LANES), lambda i: (i, 0, 0)),
            pl.BlockSpec((tt, PEER_PICKS), lambda i: (i, 0)),
            pl.BlockSpec((PEER_NEXPERTS, HALF_SUB, LANES), lambda i: (0, 0, 0),
                         pipeline_mode=pl.Buffered(1)),
        ],
        out_specs=pl.BlockSpec((tt, PEER_PICKS), lambda i: (i, 0)),
        out_shape=jax.ShapeDtypeStruct((ntok, PEER_PICKS), F32),
        scratch_shapes=[pltpu.VMEM((PEER_PICKS * HALF_SUB, LANES), I32),
                        pltpu.VMEM((PEER_PICKS * HALF_SUB, LANES), I32),
                        pltpu.VMEM((tt, STAGE_ROWS), F32)],
        compiler_params=pltpu.CompilerParams(
            dimension_semantics=("arbitrary",), vmem_limit_bytes=VMEM_TABLE_LIMIT),
        name="peer_down_gather",
    )(e, h2.reshape(ntok, SUBLANES, LANES), g, tab)


def _peer_v_kernel(tt, final, e_ref, w_ref, x_ref, g2_ref, nf_ref, tab_ref, o_ref, stage0, stage1, wrep_ref):
    mask = _pair_pattern(SUBLANES)
    stages = (stage0, stage1)
    w = w_ref[...]
    w_hi, w_lo = _split_bf16(w)
    w_lo2 = (w - w_hi.astype(F32) - w_lo.astype(F32)).astype(BF16)
    rep = _pick_of_stage_row((PEER_PICKS, STAGE_ROWS), 0)
    wrep_ref[...] = _dot(w_hi, rep) + _dot(w_lo, rep) + _dot(w_lo2, rep)
    ones = jnp.ones((LANES, LANES), BF16)

    def body(tg, carry):
        base = pl.multiple_of(tg * GROUP, GROUP)
        wr8 = wrep_ref[pl.ds(base, GROUP), :]
        xs = []
        for i in range(GROUP):
            t = base + i
            xb = _stage_rows(e_ref, t, tab_ref, stages[i % 2])
            l_hi, l_lo = _split_bf16(jnp.where(mask, wr8[i:i + 1, :], 0.0))
            out = _dot(jnp.concatenate([l_hi, l_lo], axis=0), xb)
            x2 = x_ref[t] + g2_ref[0] * (out[:SUBLANES] + out[SUBLANES:])
            if final:
                xs.append(x2)
            else:
                o_ref[t] = x2
        if final:
            sq = jnp.concatenate([jnp.sum(x2 * x2, axis=0, keepdims=True) for x2 in xs], axis=0)
            s_hi, s_lo = _split_bf16(sq)
            s_lo2 = (sq - s_hi.astype(F32) - s_lo.astype(F32)).astype(BF16)
            tot = _dot(s_hi, ones) + _dot(s_lo, ones) + _dot(s_lo2, ones)
            scale = lax.rsqrt(tot * (1.0 / D_MODEL) + EPS)
            for i in range(GROUP):
                o_ref[base + i] = xs[i] * scale[i:i + 1, :] * nf_ref[...]
        return carry

    lax.fori_loop(0, tt // GROUP, body, 0)


def _peer_v(e, w, x1, g2, nfinal, tab, tt, final):
    bsz, t, d = x1.shape
    ntok = bsz * t
    per_seq = t // tt
    return pl.pallas_call(
        functools.partial(_peer_v_kernel, tt, final),
        grid=(ntok // tt,),
        in_specs=[
            pl.BlockSpec((tt, PEER_PICKS), lambda i: (i, 0), memory_space=pltpu.SMEM),
            pl.BlockSpec((tt, PEER_PICKS), lambda i: (i, 0)),
            pl.BlockSpec((tt, SUBLANES, LANES), lambda i: (i, 0, 0)),
            pl.BlockSpec((1, SUBLANES, LANES), lambda i: (i // per_seq, 0, 0)),
            pl.BlockSpec((SUBLANES, LANES), lambda i: (0, 0)),
            pl.BlockSpec((PEER_NEXPERTS, HALF_SUB, LANES), lambda i: (0, 0, 0),
                         pipeline_mode=pl.Buffered(1)),
        ],
        out_specs=pl.BlockSpec((tt, SUBLANES, LANES), lambda i: (i, 0, 0)),
        out_shape=jax.ShapeDtypeStruct((ntok, SUBLANES, LANES), F32),
        scratch_shapes=[pltpu.VMEM((PEER_PICKS * HALF_SUB, LANES), I32),
                        pltpu.VMEM((PEER_PICKS * HALF_SUB, LANES), I32),
                        pltpu.VMEM((tt, STAGE_ROWS), F32)],
        compiler_params=pltpu.CompilerParams(
            dimension_semantics=("arbitrary",), vmem_limit_bytes=VMEM_TABLE_LIMIT),
        name="peer_up_gather",
    )(e, w, x1.reshape(ntok, SUBLANES, LANES), g2.reshape(bsz, SUBLANES, LANES),
      nfinal.reshape(SUBLANES, LANES), tab).reshape(bsz, t, d)


def _peer_ffn(x1, h2, g2, nfinal, pp, final):
    bsz, t, d = x1.shape
    ntok = bsz * t
    tq = 256
    et, gt = _peer_q(h2.reshape(ntok, d), pp["w_q"], pp["k1"], pp["k2"], tq)
    e = et.transpose(2, 0, 1).reshape(ntok, PEER_PICKS)
    g = gt.transpose(2, 0, 1).reshape(ntok, PEER_PICKS)
    w = _peer_u(e, h2.reshape(ntok, d), g, pp["u"], min(ntok, 128))
    return _peer_v(e, w, x1, g2, nfinal, pp["v"], min(t, 128), final)


def _trunk(x, mod, lru_h, lru_conv, dwconv, prm):
    bsz, t, _ = x.shape
    tt = min(t, 256)
    m0, m1 = mod[0], mod[1]
    x1, h2, new_h, new_conv = _lru_layer(x, m0, prm["norm_mix"][0], prm["norm_ffn"][0], prm["lru"],
                                         lru_h, lru_conv, tt)
    x2 = _peer_ffn(x1, h2, m0[:, 5], prm["norm_final"], prm["peer"][0], False)
    x3, h4, new_dw = _conf_layer(x2, m1, prm["norm_mix"][1], prm["norm_ffn"][1], prm["cf"], dwconv, tt)
    y = _peer_ffn(x3, h4, m1[:, 5], prm["norm_final"], prm["peer"][1], True)
    return y, new_h, new_conv, new_dw


def kernel(x_prompt, x_sample, c_prompt, c_sample, state_lru_h, state_lru_conv, state_dwconv, ada_w, ada_b, norm_mix, norm_ffn, norm_final, lru_w_in, lru_b_in, lru_conv_w, lru_conv_b, lru_gate_a_w, lru_gate_a_b, lru_gate_x_w, lru_gate_x_b, lru_lambda, lru_w_out, lru_b_out, cf_w_pw1, cf_b_pw1, cf_dw_w, cf_dw_b, cf_ln_g, cf_ln_b, cf_w_pw2, cf_b_pw2, peer_w_q, peer_k1, peer_k2, peer_u, peer_v):
    bp, bs = x_prompt.shape[0], x_sample.shape[0]
    depth = ada_w.shape[0]
    d = D_MODEL
    row = lambda a: a.reshape(1, -1)
    prm = dict(
        norm_mix=[row(norm_mix[l]) for l in range(depth)],
        norm_ffn=[row(norm_ffn[l]) for l in range(depth)],
        norm_final=norm_final,
        lru=dict(w_in=lru_w_in[0].astype(BF16), b_in=row(lru_b_in[0]), conv_w=lru_conv_w[0],
                 conv_b=row(lru_conv_b[0]), ga_w=lru_gate_a_w[0].astype(BF16), ga_b=row(lru_gate_a_b[0]),
                 gx_w=lru_gate_x_w[0].astype(BF16), gx_b=row(lru_gate_x_b[0]), lam=row(lru_lambda[0]),
                 w_out=lru_w_out[0].astype(BF16), b_out=row(lru_b_out[0])),
        cf=dict(w_pw1=cf_w_pw1[0].astype(BF16), b_pw1=row(cf_b_pw1[0]), dw_w=cf_dw_w[0],
                dw_b=row(cf_dw_b[0]), ln_g=row(cf_ln_g[0]), ln_b=row(cf_ln_b[0]),
                w_pw2=cf_w_pw2[0].astype(BF16), b_pw2=row(cf_b_pw2[0])),
        peer=[dict(w_q=peer_w_q[l].astype(BF16), k1=peer_k1[l].astype(BF16), k2=peer_k2[l].astype(BF16),
                   u=_pack_table(peer_u[l]), v=_pack_table(peer_v[l])) for l in range(depth)],
    )
    mod = _ada(jnp.concatenate([c_prompt, c_sample], axis=0), ada_w, ada_b)
    mod = mod.reshape(depth, bp + bs, 6, d)
    dt = x_prompt.dtype
    zero_h = jnp.zeros((bp, 1, LRU_WIDTH), dt)
    zero_c = jnp.zeros((bp, LRU_CONV_WIDTH - 1, LRU_WIDTH), dt)
    zero_d = jnp.zeros((bp, CF_CONV_WIDTH - 1, d), dt)
    y_p, h_p, c_p, d_p = _trunk(x_prompt, mod[:, :bp], zero_h, zero_c, zero_d, prm)
    y_s, h_s, c_s, d_s = _trunk(x_sample, mod[:, bp:], state_lru_h[0][:, None, :], state_lru_conv[0],
                                state_dwconv[0], prm)
    return (y_p, y_s, h_p.reshape(1, bp, LRU_WIDTH), c_p[None], d_p[None],
            h_s.reshape(1, bs, LRU_WIDTH), c_s[None], d_s[None])
```

```python
import functools

import jax
import jax.numpy as jnp
from jax import lax
from jax.experimental import pallas as pl
from jax.experimental.pallas import tpu as pltpu

F32 = jnp.float32
BF16 = jnp.bfloat16
I32 = jnp.int32

D_MODEL = 1024
LRU_WIDTH = 1024
LRU_HEADS = 8
LRU_BLOCK = LRU_WIDTH // LRU_HEADS
LRU_CONV_WIDTH = 4
LRU_C = 8.0
CF_CONV_WIDTH = 31
PEER_HEADS = 8
PEER_NKEYS = 128
PEER_NEXPERTS = PEER_NKEYS * PEER_NKEYS
PEER_DK = 256
PEER_TOPK = 16
PEER_PICKS = PEER_HEADS * PEER_TOPK
EPS = 1e-6

LANES = 128
SUBLANES = 8
HALF_WORDS = D_MODEL // 2
HALF_SUB = HALF_WORDS // LANES
STAGE_ROWS = PEER_PICKS * 2 * HALF_SUB
GROUP = 16
VMEM_TABLE_LIMIT = 48 * 1024 * 1024
VMEM_MIXER_LIMIT = 56 * 1024 * 1024


def _rms_mod(x, gn, sh, sc):
    ms = jnp.mean(x * x, axis=-1, keepdims=True)
    y = x * lax.rsqrt(ms + EPS) * gn
    return y * (1.0 + sc) + sh


def _split_bf16(a):
    hi = a.astype(BF16)
    lo = (a - hi.astype(F32)).astype(BF16)
    return hi, lo


def _dot(a, b):
    return jnp.dot(a, b, preferred_element_type=F32)


def _ada_kernel(c_ref, w_ref, b_ref, o_ref):
    c = c_ref[...]
    a = c * jax.nn.sigmoid(c)
    a_hi, a_lo = _split_bf16(a)
    w_hi, w_lo = _split_bf16(w_ref[0])
    o_ref[0] = _dot(a_hi, w_hi) + _dot(a_lo, w_hi) + _dot(a_hi, w_lo) + b_ref[0]


def _ada(c_all, ada_w, ada_b):
    nl, d, n6 = ada_w.shape
    nb = c_all.shape[0]
    tn = 1024
    return pl.pallas_call(
        _ada_kernel,
        grid=(nl, n6 // tn),
        in_specs=[
            pl.BlockSpec((nb, d), lambda l, j: (0, 0)),
            pl.BlockSpec((1, d, tn), lambda l, j: (l, 0, j)),
            pl.BlockSpec((1, 1, tn), lambda l, j: (l, 0, j)),
        ],
        out_specs=pl.BlockSpec((1, nb, tn), lambda l, j: (l, 0, j)),
        out_shape=jax.ShapeDtypeStruct((nl, nb, n6), F32),
        name="ada_mod",
    )(c_all, ada_w, ada_b.reshape(nl, 1, n6))


def _lin_scan(a, b, tt):
    row = lax.broadcasted_iota(I32, a.shape, 0)
    s = 1
    while s < tt:
        keep = row >= s
        a_sh = jnp.where(keep, pltpu.roll(a, s, 0), 1.0)
        b_sh = jnp.where(keep, pltpu.roll(b, s, 0), 0.0)
        b = a * b_sh + b
        a = a * a_sh
        s *= 2
    return a, b


def _lru_kernel(tt, x_ref, mod_ref, nmix_ref, nffn_ref, win_ref, bin_ref, cw_ref, cb_ref,
                gaw_ref, gab_ref, gxw_ref, gxb_ref, lam_ref, wout_ref, bout_ref, h0_ref, c0_ref,
                x1_ref, h2_ref, newh_ref, newc_ref, hcar, win):
    j = pl.program_id(1)
    nj = pl.num_programs(1)
    hist = SUBLANES - (LRU_CONV_WIDTH - 1)

    @pl.when(j == 0)
    def _():
        hcar[...] = h0_ref[0]
        win[hist:SUBLANES, :] = c0_ref[0]

    x = x_ref[0]
    sh1, sc1, g1 = mod_ref[0, 0:1, :], mod_ref[0, 1:2, :], mod_ref[0, 2:3, :]
    sh2, sc2 = mod_ref[0, 3:4, :], mod_ref[0, 4:5, :]
    h = _rms_mod(x, nmix_ref[...], sh1, sc1)
    proj = _dot(h.astype(BF16), win_ref[...]) + bin_ref[...]
    gate = jax.nn.gelu(proj[:, :LRU_WIDTH])
    win[SUBLANES:SUBLANES + tt, :] = proj[:, LRU_WIDTH:]

    xc = cb_ref[...] + win[pl.ds(hist, tt), :] * cw_ref[0:1, :]
    for k in range(1, LRU_CONV_WIDTH):
        xc = xc + win[pl.ds(hist + k, tt), :] * cw_ref[k:k + 1, :]
    newc = win[tt + hist:tt + SUBLANES, :]
    win[hist:SUBLANES, :] = newc

    xcb = xc.astype(BF16)
    ga = jnp.concatenate(
        [_dot(xcb[:, i * LRU_BLOCK:(i + 1) * LRU_BLOCK], gaw_ref[i]) for i in range(LRU_HEADS)], axis=-1)
    gx = jnp.concatenate(
        [_dot(xcb[:, i * LRU_BLOCK:(i + 1) * LRU_BLOCK], gxw_ref[i]) for i in range(LRU_HEADS)], axis=-1)
    r = jax.nn.sigmoid(ga + gab_ref[...])
    gi = jax.nn.sigmoid(gx + gxb_ref[...])
    nl = -lam_ref[...]
    softplus = jnp.maximum(nl, 0.0) + jnp.log(1.0 + jnp.exp(-jnp.abs(nl)))
    log_a = (-LRU_C) * r * softplus
    a = jnp.exp(log_a)
    mult = jnp.sqrt(1.0 - jnp.exp(2.0 * log_a))
    bv = mult * (gi * xc)
    a_cum, hs0 = _lin_scan(a, bv, tt)
    hs = hs0 + a_cum * hcar[...]
    hcar[...] = hs[tt - 1:tt, :]

    out = _dot((hs * gate).astype(BF16), wout_ref[...]) + bout_ref[...]
    x1 = x + g1 * out
    x1_ref[0] = x1
    h2_ref[0] = _rms_mod(x1, nffn_ref[...], sh2, sc2)

    @pl.when(j == nj - 1)
    def _():
        newh_ref[0] = hcar[...]
        newc_ref[0] = newc


def _lru_layer(x, mod, nmix, nffn, p, h0, c0, tt):
    bsz, t, d = x.shape
    w = LRU_WIDTH
    cw = LRU_CONV_WIDTH
    row = lambda b, j: (b, j, 0)
    per_b = lambda b, j: (b, 0, 0)
    c2 = lambda b, j: (0, 0)
    c3 = lambda b, j: (0, 0, 0)
    return pl.pallas_call(
        functools.partial(_lru_kernel, tt),
        grid=(bsz, t // tt),
        in_specs=[
            pl.BlockSpec((1, tt, d), row),
            pl.BlockSpec((1, 6, d), per_b),
            pl.BlockSpec((1, d), c2), pl.BlockSpec((1, d), c2),
            pl.BlockSpec((d, 2 * w), c2), pl.BlockSpec((1, 2 * w), c2),
            pl.BlockSpec((cw, w), c2), pl.BlockSpec((1, w), c2),
            pl.BlockSpec((LRU_HEADS, LRU_BLOCK, LRU_BLOCK), c3), pl.BlockSpec((1, w), c2),
            pl.BlockSpec((LRU_HEADS, LRU_BLOCK, LRU_BLOCK), c3), pl.BlockSpec((1, w), c2),
            pl.BlockSpec((1, w), c2),
            pl.BlockSpec((w, d), c2), pl.BlockSpec((1, d), c2),
            pl.BlockSpec((1, 1, w), per_b), pl.BlockSpec((1, cw - 1, w), per_b),
        ],
        out_specs=[
            pl.BlockSpec((1, tt, d), row), pl.BlockSpec((1, tt, d), row),
            pl.BlockSpec((1, 1, w), per_b), pl.BlockSpec((1, cw - 1, w), per_b),
        ],
        out_shape=[
            jax.ShapeDtypeStruct((bsz, t, d), F32), jax.ShapeDtypeStruct((bsz, t, d), F32),
            jax.ShapeDtypeStruct((bsz, 1, w), F32), jax.ShapeDtypeStruct((bsz, cw - 1, w), F32),
        ],
        scratch_shapes=[pltpu.VMEM((1, w), F32), pltpu.VMEM((SUBLANES + tt, w), F32)],
        compiler_params=pltpu.CompilerParams(
            dimension_semantics=("arbitrary", "arbitrary"), vmem_limit_bytes=VMEM_MIXER_LIMIT),
        name="rglru_mixer",
    )(x, mod, nmix, nffn, p["w_in"], p["b_in"], p["conv_w"], p["conv_b"], p["ga_w"], p["ga_b"],
      p["gx_w"], p["gx_b"], p["lam"], p["w_out"], p["b_out"], h0, c0)


CF_HIST = 32


def _conf_kernel(tt, x_ref, mod_ref, nmix_ref, nffn_ref, w1_ref, b1_ref, dw_ref, db_ref,
                 lng_ref, lnb_ref, w2_ref, b2_ref, d0_ref, x1_ref, h2_ref, newd_ref, win):
    j = pl.program_id(1)
    nj = pl.num_programs(1)
    nh = CF_CONV_WIDTH - 1
    first = CF_HIST - nh

    @pl.when(j == 0)
    def _():
        win[first:CF_HIST, :] = d0_ref[0]

    x = x_ref[0]
    sh1, sc1, g1 = mod_ref[0, 0:1, :], mod_ref[0, 1:2, :], mod_ref[0, 2:3, :]
    sh2, sc2 = mod_ref[0, 3:4, :], mod_ref[0, 4:5, :]
    h = _rms_mod(x, nmix_ref[...], sh1, sc1)
    proj = _dot(h.astype(BF16), w1_ref[...]) + b1_ref[...]
    glu = proj[:, :D_MODEL] * jax.nn.sigmoid(proj[:, D_MODEL:])
    win[CF_HIST:CF_HIST + tt, :] = glu

    dcv = db_ref[...] + win[pl.ds(first, tt), :] * dw_ref[0:1, :]
    for k in range(1, CF_CONV_WIDTH):
        dcv = dcv + win[pl.ds(first + k, tt), :] * dw_ref[k:k + 1, :]
    newd = win[tt + first:tt + CF_HIST, :]
    win[first:CF_HIST, :] = newd

    mu = jnp.mean(dcv, axis=-1, keepdims=True)
    cen = dcv - mu
    var = jnp.mean(cen * cen, axis=-1, keepdims=True)
    ln = cen * lax.rsqrt(var + EPS) * lng_ref[...] + lnb_ref[...]
    act = ln * jax.nn.sigmoid(ln)
    out = _dot(act.astype(BF16), w2_ref[...]) + b2_ref[...]
    x1 = x + g1 * out
    x1_ref[0] = x1
    h2_ref[0] = _rms_mod(x1, nffn_ref[...], sh2, sc2)

    @pl.when(j == nj - 1)
    def _():
        newd_ref[0] = newd


def _conf_layer(x, mod, nmix, nffn, p, d0, tt):
    bsz, t, d = x.shape
    nh = CF_CONV_WIDTH - 1
    row = lambda b, j: (b, j, 0)
    per_b = lambda b, j: (b, 0, 0)
    c2 = lambda b, j: (0, 0)
    return pl.pallas_call(
        functools.partial(_conf_kernel, tt),
        grid=(bsz, t // tt),
        in_specs=[
            pl.BlockSpec((1, tt, d), row),
            pl.BlockSpec((1, 6, d), per_b),
            pl.BlockSpec((1, d), c2), pl.BlockSpec((1, d), c2),
            pl.BlockSpec((d, 2 * d), c2), pl.BlockSpec((1, 2 * d), c2),
            pl.BlockSpec((CF_CONV_WIDTH, d), c2), pl.BlockSpec((1, d), c2),
            pl.BlockSpec((1, d), c2), pl.BlockSpec((1, d), c2),
            pl.BlockSpec((d, d), c2), pl.BlockSpec((1, d), c2),
            pl.BlockSpec((1, nh, d), per_b),
        ],
        out_specs=[
            pl.BlockSpec((1, tt, d), row), pl.BlockSpec((1, tt, d), row),
            pl.BlockSpec((1, nh, d), per_b),
        ],
        out_shape=[
            jax.ShapeDtypeStruct((bsz, t, d), F32), jax.ShapeDtypeStruct((bsz, t, d), F32),
            jax.ShapeDtypeStruct((bsz, nh, d), F32),
        ],
        scratch_shapes=[pltpu.VMEM((CF_HIST + tt, d), F32)],
        compiler_params=pltpu.CompilerParams(
            dimension_semantics=("arbitrary", "arbitrary"), vmem_limit_bytes=VMEM_MIXER_LIMIT),
        name="conformer_mixer",
    )(x, mod, nmix, nffn, p["w_pw1"], p["b_pw1"], p["dw_w"], p["dw_b"], p["ln_g"], p["ln_b"],
      p["w_pw2"], p["b_pw2"], d0)


def _top16_rows(s, payload=None):
    nrows = s.shape[0]
    row = lax.broadcasted_iota(I32, s.shape, 0)
    vals, picked = [], []
    for _ in range(PEER_TOPK):
        m = jnp.max(s, axis=0, keepdims=True)
        idx = jnp.min(jnp.where(s == m, row, nrows), axis=0, keepdims=True)
        hit = row == idx
        vals.append(m)
        if payload is None:
            picked.append(idx)
        else:
            picked.append(jnp.sum(jnp.where(hit, payload, 0), axis=0, keepdims=True))
        s = jnp.where(hit, -jnp.inf, s)
    return jnp.concatenate(vals, axis=0), jnp.concatenate(picked, axis=0)


def _peer_q_kernel(h_ref, wq_ref, k1_ref, k2_ref, e_ref, g_ref):
    half = PEER_DK // 2
    q = _dot(h_ref[...].astype(BF16), wq_ref[...])
    nt = (((1,), (1,)), ((), ()))
    s1 = lax.dot_general(k1_ref[...], q[:, :half].astype(BF16), nt, preferred_element_type=F32)
    s2 = lax.dot_general(k2_ref[...], q[:, half:].astype(BF16), nt, preferred_element_type=F32)
    s1v, s1i = _top16_rows(s1)
    s2v, s2i = _top16_rows(s2)
    cand, cand_e = [], []
    for a in range(PEER_TOPK):
        nb = PEER_TOPK // (a + 1)
        cand.append(s1v[a:a + 1, :] + s2v[:nb, :])
        cand_e.append(s1i[a:a + 1, :] * PEER_NKEYS + s2i[:nb, :])
    n = sum(c.shape[0] for c in cand)
    pad = -n % SUBLANES
    cand.append(jnp.full((pad, s1v.shape[1]), -jnp.inf, F32))
    cand_e.append(jnp.zeros((pad, s1v.shape[1]), I32))
    sv, e = _top16_rows(jnp.concatenate(cand, axis=0), jnp.concatenate(cand_e, axis=0))
    e_ref[0] = e
    ex = jnp.exp(sv - sv[0:1, :])
    g_ref[0] = ex / jnp.sum(ex, axis=0, keepdims=True)


def _peer_q(h2, wq, k1, k2, tq):
    ntok, d = h2.shape
    return pl.pallas_call(
        _peer_q_kernel,
        grid=(ntok // tq, PEER_HEADS),
        in_specs=[
            pl.BlockSpec((tq, d), lambda i, h: (i, 0)),
            pl.BlockSpec((d, PEER_DK), lambda i, h: (0, h)),
            pl.BlockSpec((PEER_NKEYS, PEER_DK // 2), lambda i, h: (0, 0)),
            pl.BlockSpec((PEER_NKEYS, PEER_DK // 2), lambda i, h: (0, 0)),
        ],
        out_specs=[
            pl.BlockSpec((1, PEER_TOPK, tq), lambda i, h: (h, 0, i)),
            pl.BlockSpec((1, PEER_TOPK, tq), lambda i, h: (h, 0, i)),
        ],
        out_shape=[
            jax.ShapeDtypeStruct((PEER_HEADS, PEER_TOPK, ntok), I32),
            jax.ShapeDtypeStruct((PEER_HEADS, PEER_TOPK, ntok), F32),
        ],
        compiler_params=pltpu.CompilerParams(dimension_semantics=("arbitrary", "arbitrary")),
        name="peer_query_topk",
    )(h2, wq, k1, k2)


def _pack_table(tab):
    lo = lax.bitcast_convert_type(tab[:, :HALF_WORDS].astype(BF16), jnp.uint16).astype(jnp.uint32)
    hi = lax.bitcast_convert_type(tab[:, HALF_WORDS:].astype(BF16), jnp.uint16).astype(jnp.uint32)
    words = lax.bitcast_convert_type(lo | (hi << 16), I32)
    return words.reshape(tab.shape[0], HALF_SUB, LANES)


def _pair_pattern(nrows):
    probe = pltpu.bitcast(jnp.full((SUBLANES, LANES), 0x3F800000, I32), BF16)
    odd_is_hi = probe[1:2, 0:1].astype(F32) == 1.0
    m = lax.broadcasted_iota(I32, (nrows, STAGE_ROWS), 0) & (SUBLANES - 1)
    r = lax.broadcasted_iota(I32, (nrows, STAGE_ROWS), 1)
    chunk = (r >> 1) & (HALF_SUB - 1)
    par = r & 1
    half = jnp.where(odd_is_hi, par, 1 - par)
    return m == half * HALF_SUB + chunk


def _stage_rows(e_ref, t, tab_ref, stage_ref):
    for j in range(0, PEER_PICKS, 2):
        pair = jnp.concatenate([tab_ref[e_ref[t, j]], tab_ref[e_ref[t, j + 1]]], axis=0)
        stage_ref[j * HALF_SUB:(j + 2) * HALF_SUB, :] = pair
    return pltpu.bitcast(stage_ref[...], BF16)


def _pick_of_stage_row(shape, pick_axis):
    r = lax.broadcasted_iota(I32, shape, 1 - pick_axis)
    j = lax.broadcasted_iota(I32, shape, pick_axis)
    return jnp.where((r >> 3) == j, 1.0, 0.0).astype(BF16)


def _peer_u_kernel(tt, e_ref, h_ref, g_ref, tab_ref, w_ref, stage0, stage1, zs_ref):
    mask = _pair_pattern(2 * SUBLANES)
    nt = (((1,), (1,)), ((), ()))
    stages = (stage0, stage1)

    def body(tg, carry):
        base = pl.multiple_of(tg * GROUP, GROUP)
        rows = []
        for i in range(GROUP):
            t = base + i
            xb = _stage_rows(e_ref, t, tab_ref, stages[i % 2])
            h_hi, h_lo = _split_bf16(h_ref[t])
            h16 = jnp.concatenate([h_hi, h_lo], axis=0)
            out = lax.dot_general(h16, xb, nt, preferred_element_type=F32)
            m = jnp.where(mask, out, 0.0)
            rows.append(jnp.sum(m[:SUBLANES] + m[SUBLANES:], axis=0, keepdims=True))
        zs_ref[pl.ds(base, GROUP), :] = jnp.concatenate(rows, axis=0)
        return carry

    lax.fori_loop(0, tt // GROUP, body, 0)
    z_hi, z_lo = _split_bf16(zs_ref[...])
    fold = _pick_of_stage_row((STAGE_ROWS, PEER_PICKS), 1)
    z = _dot(z_hi, fold) + _dot(z_lo, fold)
    w_ref[...] = g_ref[...] * jax.nn.gelu(z)


def _peer_u(e, h2, g, tab, tt):
    ntok = e.shape[0]
    return pl.pallas_call(
        functools.partial(_peer_u_kernel, tt),
        grid=(ntok // tt,),
        in_specs=[
            pl.BlockSpec((tt, PEER_PICKS), lambda i: (i, 0), memory_space=pltpu.SMEM),
            pl.BlockSpec((tt, SUBLANES, LANES), lambda i: (i, 0, 0)),
            pl.BlockSpec((tt, PEER_PICKS), lambda i: (i, 0)),
            pl.BlockSpec((PEER_NEXPERTS, HALF_SUB, LANES), lambda i: (0, 0, 0),
                         pipeline_mode=pl.Buffered(1)),
        ],
        out_specs=pl.BlockSpec((tt, PEER_PICKS), lambda i: (i, 0)),
        out_shape=jax.ShapeDtypeStruct((ntok, PEER_PICKS), F32),
        scratch_shapes=[pltpu.VMEM((PEER_PICKS * HALF_SUB, LANES), I32),
                        pltpu.VMEM((PEER_PICKS * HALF_SUB, LANES), I32),
                        pltpu.VMEM((tt, STAGE_ROWS), F32)],
        compiler_params=pltpu.CompilerParams(
            dimension_semantics=("arbitrary",), vmem_limit_bytes=VMEM_TABLE_LIMIT),
        name="peer_down_gather",
    )(e, h2.reshape(ntok, SUBLANES, LANES), g, tab)


def _peer_v_kernel(tt, final, e_ref, w_ref, x_ref, g2_ref, nf_ref, tab_ref, o_ref, stage0, stage1, wrep_ref):
    mask = _pair_pattern(SUBLANES)
    stages = (stage0, stage1)
    w = w_ref[...]
    w_hi, w_lo = _split_bf16(w)
    w_lo2 = (w - w_hi.astype(F32) - w_lo.astype(F32)).astype(BF16)
    rep = _pick_of_stage_row((PEER_PICKS, STAGE_ROWS), 0)
    wrep_ref[...] = _dot(w_hi, rep) + _dot(w_lo, rep) + _dot(w_lo2, rep)
    ones = jnp.ones((LANES, LANES), BF16)

    def body(tg, carry):
        base = pl.multiple_of(tg * GROUP, GROUP)
        wr8 = wrep_ref[pl.ds(base, GROUP), :]
        xs = []
        for i in range(GROUP):
            t = base + i
            xb = _stage_rows(e_ref, t, tab_ref, stages[i % 2])
            l_hi, l_lo = _split_bf16(jnp.where(mask, wr8[i:i + 1, :], 0.0))
            out = _dot(jnp.concatenate([l_hi, l_lo], axis=0), xb)
            x2 = x_ref[t] + g2_ref[0] * (out[:SUBLANES] + out[SUBLANES:])
            if final:
                xs.append(x2)
            else:
                o_ref[t] = x2
        if final:
            sq = jnp.concatenate([jnp.sum(x2 * x2, axis=0, keepdims=True) for x2 in xs], axis=0)
            s_hi, s_lo = _split_bf16(sq)
            s_lo2 = (sq - s_hi.astype(F32) - s_lo.astype(F32)).astype(BF16)
            tot = _dot(s_hi, ones) + _dot(s_lo, ones) + _dot(s_lo2, ones)
            scale = lax.rsqrt(tot * (1.0 / D_MODEL) + EPS)
            for i in range(GROUP):
                o_ref[base + i] = xs[i] * scale[i:i + 1, :] * nf_ref[...]
        return carry

    lax.fori_loop(0, tt // GROUP, body, 0)


def _peer_v(e, w, x1, g2, nfinal, tab, tt, final):
    bsz, t, d = x1.shape
    ntok = bsz * t
    per_seq = t // tt
    return pl.pallas_call(
        functools.partial(_peer_v_kernel, tt, final),
        grid=(ntok // tt,),
        in_specs=[
            pl.BlockSpec((tt, PEER_PICKS), lambda i: (i, 0), memory_space=pltpu.SMEM),
            pl.BlockSpec((tt, PEER_PICKS), lambda i: (i, 0)),
            pl.BlockSpec((tt, SUBLANES, LANES), lambda i: (i, 0, 0)),
            pl.BlockSpec((1, SUBLANES, LANES), lambda i: (i // per_seq, 0, 0)),
            pl.BlockSpec((SUBLANES, LANES), lambda i: (0, 0)),
            pl.BlockSpec((PEER_NEXPERTS, HALF_SUB, LANES), lambda i: (0, 0, 0),
                         pipeline_mode=pl.Buffered(1)),
        ],
        out_specs=pl.BlockSpec((tt, SUBLANES, LANES), lambda i: (i, 0, 0)),
        out_shape=jax.ShapeDtypeStruct((ntok, SUBLANES, LANES), F32),
        scratch_shapes=[pltpu.VMEM((PEER_PICKS * HALF_SUB, LANES), I32),
                        pltpu.VMEM((PEER_PICKS * HALF_SUB, LANES), I32),
                        pltpu.VMEM((tt, STAGE_ROWS), F32)],
        compiler_params=pltpu.CompilerParams(
            dimension_semantics=("arbitrary",), vmem_limit_bytes=VMEM_TABLE_LIMIT),
        name="peer_up_gather",
    )(e, w, x1.reshape(ntok, SUBLANES, LANES), g2.reshape(bsz, SUBLANES, LANES),
      nfinal.reshape(SUBLANES, LANES), tab).reshape(bsz, t, d)


def _peer_ffn(x1, h2, g2, nfinal, pp, final):
    bsz, t, d = x1.shape
    ntok = bsz * t
    tq = 512
    et, gt = _peer_q(h2.reshape(ntok, d), pp["w_q"], pp["k1"], pp["k2"], tq)
    e = et.transpose(2, 0, 1).reshape(ntok, PEER_PICKS)
    g =gt.transpose(2, 0, 1).reshape(ntok, PEER_PICKS)
    w = _peer_u(e, h2.reshape(ntok, d), g, pp["u"], min(ntok, 256))
    return _peer_v(e, w, x1, g2, nfinal, pp["v"], min(t, 256), final)


def _trunk(x, mod, lru_h, lru_conv, dwconv, prm):
    bsz, t, _ = x.shape
    tt = min(t, 256)
    m0, m1 = mod[0], mod[1]
    x1, h2, new_h, new_conv = _lru_layer(x, m0, prm["norm_mix"][0], prm["norm_ffn"][0], prm["lru"],
                                         lru_h, lru_conv, tt)
    x2 = _peer_ffn(x1, h2, m0[:, 5], prm["norm_final"], prm["peer"][0], False)
    x3, h4, new_dw = _conf_layer(x2, m1, prm["norm_mix"][1], prm["norm_ffn"][1], prm["cf"], dwconv, tt)
    y = _peer_ffn(x3, h4, m1[:, 5], prm["norm_final"], prm["peer"][1], True)
    return y, new_h, new_conv, new_dw


def kernel(x_prompt, x_sample, c_prompt, c_sample, state_lru_h, state_lru_conv, state_dwconv, ada_w, ada_b, norm_mix, norm_ffn, norm_final, lru_w_in, lru_b_in, lru_conv_w, lru_conv_b, lru_gate_a_w, lru_gate_a_b, lru_gate_x_w, lru_gate_x_b, lru_lambda, lru_w_out, lru_b_out, cf_w_pw1, cf_b_pw1, cf_dw_w, cf_dw_b, cf_ln_g, cf_ln_b, cf_w_pw2, cf_b_pw2, peer_w_q, peer_k1, peer_k2, peer_u, peer_v):
    bp, bs = x_prompt.shape[0], x_sample.shape[0]
    depth = ada_w.shape[0]
    d = D_MODEL
    row = lambda a: a.reshape(1, -1)
    prm = dict(
        norm_mix=[row(norm_mix[l]) for l in range(depth)],
        norm_ffn=[row(norm_ffn[l]) for l in range(depth)],
        norm_final=norm_final,
        lru=dict(w_in=lru_w_in[0].astype(BF16), b_in=row(lru_b_in[0]), conv_w=lru_conv_w[0],
                 conv_b=row(lru_conv_b[0]), ga_w=lru_gate_a_w[0].astype(BF16), ga_b=row(lru_gate_a_b[0]),
                 gx_w=lru_gate_x_w[0].astype(BF16), gx_b=row(lru_gate_x_b[0]), lam=row(lru_lambda[0]),
                 w_out=lru_w_out[0].astype(BF16), b_out=row(lru_b_out[0])),
        cf=dict(w_pw1=cf_w_pw1[0].astype(BF16), b_pw1=row(cf_b_pw1[0]), dw_w=cf_dw_w[0],
                dw_b=row(cf_dw_b[0]), ln_g=row(cf_ln_g[0]), ln_b=row(cf_ln_b[0]),
                w_pw2=cf_w_pw2[0].astype(BF16), b_pw2=row(cf_b_pw2[0])),
        peer=[dict(w_q=peer_w_q[l].astype(BF16), k1=peer_k1[l].astype(BF16), k2=peer_k2[l].astype(BF16),
                   u=_pack_table(peer_u[l]), v=_pack_table(peer_v[l])) for l in range(depth)],
    )
    mod = _ada(jnp.concatenate([c_prompt, c_sample], axis=0), ada_w, ada_b)
    mod = mod.reshape(depth, bp + bs, 6, d)
    dt = x_prompt.dtype
    zero_h = jnp.zeros((bp, 1, LRU_WIDTH), dt)
    zero_c = jnp.zeros((bp, LRU_CONV_WIDTH - 1, LRU_WIDTH), dt)
    zero_d = jnp.zeros((bp, CF_CONV_WIDTH - 1, d), dt)
    y_p, h_p, c_p, d_p = _trunk(x_prompt, mod[:, :bp], zero_h, zero_c, zero_d, prm)
    y_s, h_s, c_s, d_s = _trunk(x_sample, mod[:, bp:], state_lru_h[0][:, None, :], state_lru_conv[0],
                                state_dwconv[0], prm)
    return (y_p, y_s, h_p.reshape(1, bp, LRU_WIDTH), c_p[None], d_p[None],
            h_s.reshape(1, bs, LRU_WIDTH), c_s[None], d_s[None])
```

```python
import functools

import jax
import jax.numpy as jnp
from jax import lax
from jax.experimental import pallas as pl
from jax.experimental.pallas import tpu as pltpu

F32 = jnp.float32
BF16 = jnp.bfloat16
I32 = jnp.int32

D_MODEL = 1024
LRU_WIDTH = 1024
LRU_HEADS = 8
LRU_BLOCK = LRU_WIDTH // LRU_HEADS
LRU_CONV_WIDTH = 4
LRU_C = 8.0
CF_CONV_WIDTH = 31
PEER_HEADS = 8
PEER_NKEYS = 128
PEER_NEXPERTS = PEER_NKEYS * PEER_NKEYS
PEER_DK = 256
PEER_TOPK = 16
PEER_PICKS = PEER_HEADS * PEER_TOPK
EPS = 1e-6

LANES = 128
SUBLANES = 8
HALF_WORDS = D_MODEL // 2
HALF_SUB = HALF_WORDS // LANES
STAGE_ROWS = PEER_PICKS * 2 * HALF_SUB
GROUP = 16
VMEM_TABLE_LIMIT = 48 * 1024 * 1024
VMEM_MIXER_LIMIT = 56 * 1024 * 1024


def _rms_mod(x, gn, sh, sc):
    ms = jnp.mean(x * x, axis=-1, keepdims=True)
    y = x * lax.rsqrt(ms + EPS) * gn
    return y * (1.0 + sc) + sh


def _split_bf16(a):
    hi = a.astype(BF16)
    lo = (a - hi.astype(F32)).astype(BF16)
    return hi, lo


def _dot(a, b):
    return jnp.dot(a, b, preferred_element_type=F32)


def _ada_kernel(c_ref, w_ref, b_ref, o_ref):
    c = c_ref[...]
    a = c * jax.nn.sigmoid(c)
    a_hi, a_lo = _split_bf16(a)
    w_hi, w_lo = _split_bf16(w_ref[0])
    o_ref[0] = _dot(a_hi, w_hi) + _dot(a_lo, w_hi) + _dot(a_hi, w_lo) + b_ref[0]


def _ada(c_all, ada_w, ada_b):
    nl, d, n6 = ada_w.shape
    nb = c_all.shape[0]
    tn = 1024
    return pl.pallas_call(
        _ada_kernel,
        grid=(nl, n6 // tn),
        in_specs=[
            pl.BlockSpec((nb, d), lambda l, j: (0, 0)),
            pl.BlockSpec((1, d, tn), lambda l, j: (l, 0, j)),
            pl.BlockSpec((1, 1, tn), lambda l, j: (l, 0, j)),
        ],
        out_specs=pl.BlockSpec((1, nb, tn), lambda l, j: (l, 0, j)),
        out_shape=jax.ShapeDtypeStruct((nl, nb, n6), F32),
        name="ada_mod",
    )(c_all, ada_w, ada_b.reshape(nl, 1, n6))


def _lin_scan(a, b, h_in, tt):
    sub = lax.broadcasted_iota(I32, a.shape, 0) & (SUBLANES - 1)
    s = 1
    while s < SUBLANES:
        keep = sub >= s
        a_sh = jnp.where(keep, pltpu.roll(a, s, 0), 1.0)
        b_sh = jnp.where(keep, pltpu.roll(b, s, 0), 0.0)
        b = a * b_sh + b
        a = a * a_sh
        s *= 2
    out, carry = [], h_in
    for v in range(tt // SUBLANES):
        rows = slice(v * SUBLANES, (v + 1) * SUBLANES)
        hv = b[rows] + a[rows] * carry
        carry = hv[SUBLANES - 1:SUBLANES]
        out.append(hv)
    return jnp.concatenate(out, axis=0)


def _lru_kernel(tt, x_ref, mod_ref, nmix_ref, nffn_ref, win_ref, bin_ref, cw_ref, cb_ref,
                gaw_ref, gab_ref, gxw_ref, gxb_ref, lam_ref, wout_ref, bout_ref, h0_ref, c0_ref,
                x1_ref, h2_ref, newh_ref, newc_ref, hcar, win):
    j = pl.program_id(1)
    nj = pl.num_programs(1)
    hist = SUBLANES - (LRU_CONV_WIDTH - 1)

    @pl.when(j == 0)
    def _():
        hcar[...] = h0_ref[0]
        win[hist:SUBLANES, :] = c0_ref[0]

    x = x_ref[0]
    sh1, sc1, g1 = mod_ref[0, 0:1, :], mod_ref[0, 1:2, :], mod_ref[0, 2:3, :]
    sh2, sc2 = mod_ref[0, 3:4, :], mod_ref[0, 4:5, :]
    h = _rms_mod(x, nmix_ref[...], sh1, sc1)
    proj = _dot(h.astype(BF16), win_ref[...]) + bin_ref[...]
    gate = jax.nn.gelu(proj[:, :LRU_WIDTH])
    win[SUBLANES:SUBLANES + tt, :] = proj[:, LRU_WIDTH:]

    xc = cb_ref[...] + win[pl.ds(hist, tt), :] * cw_ref[0:1, :]
    for k in range(1, LRU_CONV_WIDTH):
        xc = xc + win[pl.ds(hist + k, tt), :] * cw_ref[k:k + 1, :]
    newc = win[tt + hist:tt + SUBLANES, :]
    win[hist:SUBLANES, :] = newc

    xcb = xc.astype(BF16)
    ga = jnp.concatenate(
        [_dot(xcb[:, i * LRU_BLOCK:(i + 1) * LRU_BLOCK], gaw_ref[i]) for i in range(LRU_HEADS)], axis=-1)
    gx = jnp.concatenate(
        [_dot(xcb[:, i * LRU_BLOCK:(i + 1) * LRU_BLOCK], gxw_ref[i]) for i in range(LRU_HEADS)], axis=-1)
    r = jax.nn.sigmoid(ga + gab_ref[...])
    gi = jax.nn.sigmoid(gx + gxb_ref[...])
    nl = -lam_ref[...]
    softplus = jnp.maximum(nl, 0.0) + jnp.log(1.0 + jnp.exp(-jnp.abs(nl)))
    log_a = (-LRU_C) * r * softplus
    a = jnp.exp(log_a)
    mult = jnp.sqrt(1.0 - jnp.exp(2.0 * log_a))
    bv = mult * (gi * xc)
    hs = _lin_scan(a, bv, hcar[...], tt)
    hcar[...] = hs[tt - 1:tt, :]

    out = _dot((hs * gate).astype(BF16), wout_ref[...]) + bout_ref[...]
    x1 = x + g1 * out
    x1_ref[0] = x1
    h2_ref[0] = _rms_mod(x1, nffn_ref[...], sh2, sc2)

    @pl.when(j == nj - 1)
    def _():
        newh_ref[0] = hcar[...]
        newc_ref[0] = newc


def _lru_layer(x, mod, nmix, nffn, p, h0, c0, tt):
    bsz, t, d = x.shape
    w = LRU_WIDTH
    cw = LRU_CONV_WIDTH
    row = lambda b, j: (b, j, 0)
    per_b = lambda b, j: (b, 0, 0)
    c2 = lambda b, j: (0, 0)
    c3 = lambda b, j: (0, 0, 0)
    return pl.pallas_call(
        functools.partial(_lru_kernel, tt),
        grid=(bsz, t // tt),
        in_specs=[
            pl.BlockSpec((1, tt, d), row),
            pl.BlockSpec((1, 6, d), per_b),
            pl.BlockSpec((1, d), c2), pl.BlockSpec((1, d), c2),
            pl.BlockSpec((d, 2 * w), c2), pl.BlockSpec((1, 2 * w), c2),
            pl.BlockSpec((cw, w), c2), pl.BlockSpec((1, w), c2),
            pl.BlockSpec((LRU_HEADS, LRU_BLOCK, LRU_BLOCK), c3), pl.BlockSpec((1, w), c2),
            pl.BlockSpec((LRU_HEADS, LRU_BLOCK, LRU_BLOCK), c3), pl.BlockSpec((1, w), c2),
            pl.BlockSpec((1, w), c2),
            pl.BlockSpec((w, d), c2), pl.BlockSpec((1, d), c2),
            pl.BlockSpec((1, 1, w), per_b), pl.BlockSpec((1, cw - 1, w), per_b),
        ],
        out_specs=[
            pl.BlockSpec((1, tt, d), row), pl.BlockSpec((1, tt, d), row),
            pl.BlockSpec((1, 1, w), per_b), pl.BlockSpec((1, cw - 1, w), per_b),
        ],
        out_shape=[
            jax.ShapeDtypeStruct((bsz, t, d), F32), jax.ShapeDtypeStruct((bsz, t, d), F32),
            jax.ShapeDtypeStruct((bsz, 1, w), F32), jax.ShapeDtypeStruct((bsz, cw - 1, w), F32),
        ],
        scratch_shapes=[pltpu.VMEM((1, w), F32), pltpu.VMEM((SUBLANES + tt, w), F32)],
        compiler_params=pltpu.CompilerParams(
            dimension_semantics=("arbitrary", "arbitrary"), vmem_limit_bytes=VMEM_MIXER_LIMIT),
        name="rglru_mixer",
    )(x, mod, nmix, nffn, p["w_in"], p["b_in"], p["conv_w"], p["conv_b"], p["ga_w"], p["ga_b"],
      p["gx_w"], p["gx_b"], p["lam"], p["w_out"], p["b_out"], h0, c0)


CF_HIST = 32


def _conf_kernel(tt, x_ref, mod_ref, nmix_ref, nffn_ref, w1_ref, b1_ref, dw_ref, db_ref,
                 lng_ref, lnb_ref, w2_ref, b2_ref, d0_ref, x1_ref, h2_ref, newd_ref, win, shf):
    j = pl.program_id(1)
    nj = pl.num_programs(1)
    nh = CF_CONV_WIDTH - 1
    first = CF_HIST - nh

    @pl.when(j == 0)
    def _():
        win[first:CF_HIST, :] = d0_ref[0]

    x = x_ref[0]
    sh1, sc1, g1 = mod_ref[0, 0:1, :], mod_ref[0, 1:2, :], mod_ref[0, 2:3, :]
    sh2, sc2 = mod_ref[0, 3:4, :], mod_ref[0, 4:5, :]
    h = _rms_mod(x, nmix_ref[...], sh1, sc1)
    proj = _dot(h.astype(BF16), w1_ref[...]) + b1_ref[...]
    glu = proj[:, :D_MODEL] * jax.nn.sigmoid(proj[:, D_MODEL:])
    win[CF_HIST:CF_HIST + tt, :] = glu

    for r in range(1, SUBLANES):
        shf[r - 1] = win[pl.ds(r, CF_HIST + tt - SUBLANES), :]
    dcv = db_ref[...]
    for k in range(CF_CONV_WIDTH):
        q, r = divmod(first + k, SUBLANES)
        rows = pl.ds(q * SUBLANES, tt)
        tap = win[rows, :] if r == 0 else shf[r - 1, rows, :]
        dcv = dcv + tap * dw_ref[k:k + 1, :]
    newd = win[tt + first:tt + CF_HIST, :]
    win[first:CF_HIST, :] = newd

    mu = jnp.mean(dcv, axis=-1, keepdims=True)
    cen = dcv - mu
    var = jnp.mean(cen * cen, axis=-1, keepdims=True)
    ln = cen * lax.rsqrt(var + EPS) * lng_ref[...] + lnb_ref[...]
    act = ln * jax.nn.sigmoid(ln)
    out = _dot(act.astype(BF16), w2_ref[...]) + b2_ref[...]
    x1 = x + g1 * out
    x1_ref[0] = x1
    h2_ref[0] = _rms_mod(x1, nffn_ref[...], sh2, sc2)

    @pl.when(j == nj - 1)
    def _():
        newd_ref[0] = newd


def _conf_layer(x, mod, nmix, nffn, p, d0, tt):
    bsz, t, d = x.shape
    nh = CF_CONV_WIDTH - 1
    row = lambda b, j: (b, j, 0)
    per_b = lambda b, j: (b, 0, 0)
    c2 = lambda b, j: (0, 0)
    return pl.pallas_call(
        functools.partial(_conf_kernel, tt),
        grid=(bsz, t // tt),
        in_specs=[
            pl.BlockSpec((1, tt, d), row),
            pl.BlockSpec((1, 6, d), per_b),
            pl.BlockSpec((1, d), c2), pl.BlockSpec((1, d), c2),
            pl.BlockSpec((d, 2 * d), c2), pl.BlockSpec((1, 2 * d), c2),
            pl.BlockSpec((CF_CONV_WIDTH, d), c2), pl.BlockSpec((1, d), c2),
            pl.BlockSpec((1, d), c2), pl.BlockSpec((1, d), c2),
            pl.BlockSpec((d, d), c2), pl.BlockSpec((1, d), c2),
            pl.BlockSpec((1, nh, d), per_b),
        ],
        out_specs=[
            pl.BlockSpec((1, tt, d), row), pl.BlockSpec((1, tt, d), row),
            pl.BlockSpec((1, nh, d), per_b),
        ],
        out_shape=[
            jax.ShapeDtypeStruct((bsz, t, d), F32), jax.ShapeDtypeStruct((bsz, t, d), F32),
            jax.ShapeDtypeStruct((bsz, nh, d), F32),
        ],
        scratch_shapes=[pltpu.VMEM((CF_HIST + tt, d), F32),
                        pltpu.VMEM((SUBLANES - 1, CF_HIST + tt - SUBLANES, d), F32)],
        compiler_params=pltpu.CompilerParams(
            dimension_semantics=("arbitrary", "arbitrary"), vmem_limit_bytes=VMEM_MIXER_LIMIT),
        name="conformer_mixer",
    )(x, mod, nmix, nffn, p["w_pw1"], p["b_pw1"], p["dw_w"], p["dw_b"], p["ln_g"], p["ln_b"],
      p["w_pw2"], p["b_pw2"], d0)


def _top16_rows(s, payload=None):
    nrows = s.shape[0]
    row = lax.broadcasted_iota(I32, s.shape, 0)
    vals, picked = [], []
    for _ in range(PEER_TOPK):
        m = jnp.max(s, axis=0, keepdims=True)
        idx = jnp.min(jnp.where(s == m, row, nrows), axis=0, keepdims=True)
        hit = row == idx
        vals.append(m)
        if payload is None:
            picked.append(idx)
        else:
            picked.append(jnp.sum(jnp.where(hit, payload, 0), axis=0, keepdims=True))
        s = jnp.where(hit, -jnp.inf, s)
    return jnp.concatenate(vals, axis=0), jnp.concatenate(picked, axis=0)


def _peer_q_kernel(h_ref, wq_ref, k1_ref, k2_ref, e_ref, g_ref):
    half = PEER_DK // 2
    q = _dot(h_ref[...].astype(BF16), wq_ref[...])
    nt = (((1,), (1,)), ((), ()))
    s1 = lax.dot_general(k1_ref[...], q[:, :half].astype(BF16), nt, preferred_element_type=F32)
    s2 = lax.dot_general(k2_ref[...], q[:, half:].astype(BF16), nt, preferred_element_type=F32)
    s1v, s1i = _top16_rows(s1)
    s2v, s2i = _top16_rows(s2)
    cand, cand_e = [], []
    for a in range(PEER_TOPK):
        nb = PEER_TOPK // (a + 1)
        cand.append(s1v[a:a + 1, :] + s2v[:nb, :])
        cand_e.append(s1i[a:a + 1, :] * PEER_NKEYS + s2i[:nb, :])
    n = sum(c.shape[0] for c in cand)
    pad = -n % SUBLANES
    cand.append(jnp.full((pad, s1v.shape[1]), -jnp.inf, F32))
    cand_e.append(jnp.zeros((pad, s1v.shape[1]), I32))
    sv, e = _top16_rows(jnp.concatenate(cand, axis=0), jnp.concatenate(cand_e, axis=0))
    e_ref[0] = e
    ex = jnp.exp(sv - sv[0:1, :])
    g_ref[0] = ex / jnp.sum(ex, axis=0, keepdims=True)


def _peer_q(h2, wq, k1, k2, tq):
    ntok, d = h2.shape
    return pl.pallas_call(
        _peer_q_kernel,
        grid=(ntok // tq, PEER_HEADS),
        in_specs=[
            pl.BlockSpec((tq, d), lambda i, h: (i, 0)),
            pl.BlockSpec((d, PEER_DK), lambda i, h: (0, h)),
            pl.BlockSpec((PEER_NKEYS, PEER_DK // 2), lambda i, h: (0, 0)),
            pl.BlockSpec((PEER_NKEYS, PEER_DK // 2), lambda i, h: (0, 0)),
        ],
        out_specs=[
            pl.BlockSpec((1, PEER_TOPK, tq), lambda i, h: (h, 0, i)),
            pl.BlockSpec((1, PEER_TOPK, tq), lambda i, h: (h, 0, i)),
        ],
        out_shape=[
            jax.ShapeDtypeStruct((PEER_HEADS, PEER_TOPK, ntok), I32),
            jax.ShapeDtypeStruct((PEER_HEADS, PEER_TOPK, ntok), F32),
        ],
        compiler_params=pltpu.CompilerParams(dimension_semantics=("arbitrary", "arbitrary")),
        name="peer_query_topk",
    )(h2, wq, k1, k2)


def _pack_table(tab):
    lo = lax.bitcast_convert_type(tab[:, :HALF_WORDS].astype(BF16), jnp.uint16).astype(jnp.uint32)
    hi = lax.bitcast_convert_type(tab[:, HALF_WORDS:].astype(BF16), jnp.uint16).astype(jnp.uint32)
    words = lax.bitcast_convert_type(lo | (hi << 16), I32)
    return words.reshape(tab.shape[0], HALF_SUB, LANES)


def _pair_pattern(nrows):
    probe = pltpu.bitcast(jnp.full((SUBLANES, LANES), 0x3F800000, I32), BF16)
    odd_is_hi = probe[1:2, 0:1].astype(F32) == 1.0
    m = lax.broadcasted_iota(I32, (nrows, STAGE_ROWS), 0) & (SUBLANES - 1)
    r = lax.broadcasted_iota(I32, (nrows, STAGE_ROWS), 1)
    chunk = (r >> 1) & (HALF_SUB - 1)
    par = r & 1
    half = jnp.where(odd_is_hi, par, 1 - par)
    return m == half * HALF_SUB + chunk


def _stage_rows(e_ref, t, tab_ref, stage_ref):
    for j in range(0, PEER_PICKS, 2):
        pair = jnp.concatenate([tab_ref[e_ref[t, j]], tab_ref[e_ref[t, j + 1]]], axis=0)
        stage_ref[j * HALF_SUB:(j + 2) * HALF_SUB, :] = pair
    return pltpu.bitcast(stage_ref[...], BF16)


def _pick_of_stage_row(shape, pick_axis):
    r = lax.broadcasted_iota(I32, shape, 1 - pick_axis)
    j = lax.broadcasted_iota(I32, shape, pick_axis)
    return jnp.where((r >> 3) == j, 1.0, 0.0).astype(BF16)


def _peer_u_kernel(tt, e_ref, h_ref, g_ref, tab_ref, w_ref, stage0, stage1, zs_ref):
    mask = _pair_pattern(2 * SUBLANES)
    nt = (((1,), (1,)), ((), ()))
    stages = (stage0, stage1)

    def body(tg, carry):
        base = pl.multiple_of(tg * GROUP, GROUP)
        rows = []
        for i in range(GROUP):
            t = base + i
            xb = _stage_rows(e_ref, t, tab_ref, stages[i % 2])
            h_hi, h_lo = _split_bf16(h_ref[t])
            h16 = jnp.concatenate([h_hi, h_lo], axis=0)
            out = lax.dot_general(h16, xb, nt, preferred_element_type=F32)
            m = jnp.where(mask, out, 0.0)
            rows.append(jnp.sum(m[:SUBLANES] + m[SUBLANES:], axis=0, keepdims=True))
        zs_ref[pl.ds(base, GROUP), :] = jnp.concatenate(rows, axis=0)
        return carry

    lax.fori_loop(0, tt // GROUP, body, 0)
    z_hi, z_lo = _split_bf16(zs_ref[...])
    fold = _pick_of_stage_row((STAGE_ROWS, PEER_PICKS), 1)
    z = _dot(z_hi, fold) + _dot(z_lo, fold)
    w_ref[...] = g_ref[...] * jax.nn.gelu(z)


def _peer_u(e, h2, g, tab, tt):
    ntok = e.shape[0]
    return pl.pallas_call(
        functools.partial(_peer_u_kernel, tt),
        grid=(ntok // tt,),
        in_specs=[
            pl.BlockSpec((tt, PEER_PICKS), lambda i: (i, 0), memory_space=pltpu.SMEM),
            pl.BlockSpec((tt, SUBLANES, LANES), lambda i: (i, 0, 0)),
            pl.BlockSpec((tt, PEER_PICKS), lambda i: (i, 0)),
            pl.BlockSpec((PEER_NEXPERTS, HALF_SUB, LANES), lambda i: (0, 0, 0),
                         pipeline_mode=pl.Buffered(1)),
        ],
        out_specs=pl.BlockSpec((tt, PEER_PICKS), lambda i: (i, 0)),
        out_shape=jax.ShapeDtypeStruct((ntok, PEER_PICKS), F32),
        scratch_shapes=[pltpu.VMEM((PEER_PICKS * HALF_SUB, LANES), I32),
                        pltpu.VMEM((PEER_PICKS * HALF_SUB, LANES), I32),
                        pltpu.VMEM((tt, STAGE_ROWS), F32)],
        compiler_params=pltpu.CompilerParams(
            dimension_semantics=("arbitrary",), vmem_limit_bytes=VMEM_TABLE_LIMIT),
        name="peer_down_gather",
    )(e, h2.reshape(ntok, SUBLANES, LANES), g, tab)


def _peer_v_kernel(tt, final, e_ref, w_ref, x_ref, g2_ref, nf_ref, tab_ref, o_ref, stage0, stage1, wrep_ref):
    mask = _pair_pattern(SUBLANES)
    stages = (stage0, stage1)
    w = w_ref[...]
    w_hi, w_lo = _split_bf16(w)
    w_lo2 = (w - w_hi.astype(F32) - w_lo.astype(F32)).astype(BF16)
    rep = _pick_of_stage_row((PEER_PICKS, STAGE_ROWS), 0)
    wrep_ref[...] = _dot(w_hi, rep) + _dot(w_lo, rep) + _dot(w_lo2, rep)
    ones = jnp.ones((LANES, LANES), BF16)

    def body(tg, carry):
        base = pl.multiple_of(tg * GROUP, GROUP)
        wr8 = wrep_ref[pl.ds(base, GROUP), :]
        xs = []
        for i in range(GROUP):
            t = base + i
            xb = _stage_rows(e_ref, t, tab_ref, stages[i % 2])
            l_hi, l_lo = _split_bf16(jnp.where(mask, wr8[i:i + 1, :], 0.0))
            out = _dot(jnp.concatenate([l_hi, l_lo], axis=0), xb)
            x2 = x_ref[t] + g2_ref[0] * (out[:SUBLANES] + out[SUBLANES:])
            if final:
                xs.append(x2)
            else:
                o_ref[t] = x2
        if final:
            sq = jnp.concatenate([jnp.sum(x2 * x2, axis=0, keepdims=True) for x2 in xs], axis=0)
            s_hi, s_lo = _split_bf16(sq)
            s_lo2 = (sq - s_hi.astype(F32) - s_lo.astype(F32)).astype(BF16)
            tot = _dot(s_hi, ones) + _dot(s_lo, ones) + _dot(s_lo2, ones)
            scale = lax.rsqrt(tot * (1.0 / D_MODEL) + EPS)
            for i in range(GROUP):
                o_ref[base + i] = xs[i] * scale[i:i + 1, :] * nf_ref[...]
        return carry

    lax.fori_loop(0, tt // GROUP, body, 0)


def _peer_v(e, w, x1, g2, nfinal, tab, tt, final):
    bsz, t, d = x1.shape
    ntok = bsz * t
    per_seq = t // tt
    return pl.pallas_call(
        functools.partial(_peer_v_kernel, tt, final),
        grid=(ntok // tt,),
        in_specs=[
            pl.BlockSpec((tt, PEER_PICKS), lambda i: (i, 0), memory_space=pltpu.SMEM),
            pl.BlockSpec((tt, PEER_PICKS), lambda i: (i, 0)),
            pl.BlockSpec((tt, SUBLANES, LANES), lambda i: (i, 0, 0)),
            pl.BlockSpec((1, SUBLANES, LANES), lambda i: (i // per_seq, 0, 0)),
            pl.BlockSpec((SUBLANES, LANES), lambda i: (0, 0)),
            pl.BlockSpec((PEER_NEXPERTS, HALF_SUB, LANES), lambda i: (0, 0, 0),
                         pipeline_mode=pl.Buffered(1)),
        ],
        out_specs=pl.BlockSpec((tt, SUBLANES, LANES), lambda i: (i, 0, 0)),
        out_shape=jax.ShapeDtypeStruct((ntok, SUBLANES, LANES), F32),
        scratch_shapes=[pltpu.VMEM((PEER_PICKS * HALF_SUB, LANES), I32),
                        pltpu.VMEM((PEER_PICKS * HALF_SUB, LANES), I32),
                        pltpu.VMEM((tt, STAGE_ROWS), F32)],
        compiler_params=pltpu.CompilerParams(
            dimension_semantics=("arbitrary",), vmem_limit_bytes=VMEM_TABLE_LIMIT),
        name="peer_up_gather",
    )(e, w, x1.reshape(ntok, SUBLANES, LANES), g2.reshape(bsz, SUBLANES, LANES),
      nfinal.reshape(SUBLANES, LANES), tab).reshape(bsz, t, d)


def _peer_ffn(x1, h2, g2, nfinal, pp, final):
    bsz, t, d = x1.shape
    ntok = bsz * t
    tq = 512
    et, gt = _peer_q(h2.reshape(ntok, d), pp["w_q"], pp["k1"], pp["k2"], tq)
    e = et.transpose(2, 0, 1).reshape(ntok, PEER_PICKS)
    g = gt.transpose(2, 0, 1).reshape(ntok, PEER_PICKS)
    w = _peer_u(e, h2.reshape(ntok, d), g, pp["u"], min(ntok, 256))
    return _peer_v(e, w, x1, g2, nfinal, pp["v"], min(t, 256), final)


def _trunk(x, mod, lru_h, lru_conv, dwconv, prm):
    bsz, t, _ = x.shape
    tt = min(t, 256)
    m0, m1 = mod[0], mod[1]
    x1, h2, new_h, new_conv = _lru_layer(x, m0, prm["norm_mix"][0], prm["norm_ffn"][0], prm["lru"],
                                         lru_h, lru_conv, tt)
    x2 = _peer_ffn(x1, h2, m0[:, 5], prm["norm_final"], prm["peer"][0], False)
    x3, h4, new_dw = _conf_layer(x2, m1, prm["norm_mix"][1], prm["norm_ffn"][1], prm["cf"], dwconv, tt)
    y = _peer_ffn(x3, h4, m1[:, 5], prm["norm_final"], prm["peer"][1], True)
    return y, new_h, new_conv, new_dw


def kernel(x_prompt, x_sample, c_prompt, c_sample, state_lru_h, state_lru_conv, state_dwconv, ada_w, ada_b, norm_mix, norm_ffn, norm_final, lru_w_in, lru_b_in, lru_conv_w, lru_conv_b, lru_gate_a_w, lru_gate_a_b, lru_gate_x_w, lru_gate_x_b, lru_lambda, lru_w_out, lru_b_out, cf_w_pw1, cf_b_pw1, cf_dw_w, cf_dw_b, cf_ln_g, cf_ln_b, cf_w_pw2, cf_b_pw2, peer_w_q, peer_k1, peer_k2, peer_u, peer_v):
    bp, bs = x_prompt.shape[0], x_sample.shape[0]
    depth = ada_w.shape[0]
    d = D_MODEL
    row = lambda a: a.reshape(1, -1)
    prm = dict(
        norm_mix=[row(norm_mix[l]) for l in range(depth)],
        norm_ffn=[row(norm_ffn[l]) for l in range(depth)],
        norm_final=norm_final,
        lru=dict(w_in=lru_w_in[0].astype(BF16), b_in=row(lru_b_in[0]), conv_w=lru_conv_w[0],
                 conv_b=row(lru_conv_b[0]), ga_w=lru_gate_a_w[0].astype(BF16), ga_b=row(lru_gate_a_b[0]),
                 gx_w=lru_gate_x_w[0].astype(BF16), gx_b=row(lru_gate_x_b[0]), lam=row(lru_lambda[0]),
                 w_out=lru_w_out[0].astype(BF16), b_out=row(lru_b_out[0])),
        cf=dict(w_pw1=cf_w_pw1[0].astype(BF16), b_pw1=row(cf_b_pw1[0]), dw_w=cf_dw_w[0],
                dw_b=row(cf_dw_b[0]), ln_g=row(cf_ln_g[0]), ln_b=row(cf_ln_b[0]),
                w_pw2=cf_w_pw2[0].astype(BF16), b_pw2=row(cf_b_pw2[0])),
        peer=[dict(w_q=peer_w_q[l].astype(BF16), k1=peer_k1[l].astype(BF16), k2=peer_k2[l].astype(BF16),
                   u=_pack_table(peer_u[l]), v=_pack_table(peer_v[l])) for l in range(depth)],
    )
    mod = _ada(jnp.concatenate([c_prompt, c_sample], axis=0), ada_w, ada_b)
    mod = mod.reshape(depth, bp + bs, 6, d)
    dt = x_prompt.dtype
    zero_h = jnp.zeros((bp, 1, LRU_WIDTH), dt)
    zero_c = jnp.zeros((bp, LRU_CONV_WIDTH - 1, LRU_WIDTH), dt)
    zero_d = jnp.zeros((bp, CF_CONV_WIDTH - 1, d), dt)
    y_p, h_p, c_p, d_p = _trunk(x_prompt, mod[:, :bp], zero_h, zero_c, zero_d, prm)
    y_s, h_s, c_s, d_s = _trunk(x_sample, mod[:, bp:], state_lru_h[0][:, None, :], state_lru_conv[0],
                                state_dwconv[0], prm)
    return (y_p, y_s, h_p.reshape(1, bp, LRU_WIDTH), c_p[None], d_p[None],
            h_s.reshape(1, bs, LRU_WIDTH), c_s[None], d_s[None])
```

```python
import functools

import jax
import jax.numpy as jnp
from jax import lax
from jax.experimental import pallas as pl
from jax.experimental.pallas import tpu as pltpu

F32 = jnp.float32
BF16 = jnp.bfloat16
I32 = jnp.int32

D_MODEL = 1024
LRU_WIDTH = 1024
LRU_HEADS = 8
LRU_BLOCK = LRU_WIDTH // LRU_HEADS
LRU_CONV_WIDTH = 4
LRU_C = 8.0
CF_CONV_WIDTH = 31
PEER_HEADS = 8
PEER_NKEYS = 128
PEER_NEXPERTS = PEER_NKEYS * PEER_NKEYS
PEER_DK = 256
PEER_TOPK = 16
PEER_PICKS = PEER_HEADS * PEER_TOPK
EPS = 1e-6

LANES = 128
SUBLANES = 8
HALF_WORDS = D_MODEL // 2
HALF_SUB = HALF_WORDS // LANES
STAGE_ROWS = PEER_PICKS * 2 * HALF_SUB
GROUP = 32
VMEM_TABLE_LIMIT = 48 * 1024 * 1024
VMEM_MIXER_LIMIT = 56 * 1024 * 1024


def _rms_mod(x, gn, sh, sc):
    ms = jnp.mean(x * x, axis=-1, keepdims=True)
    y = x * lax.rsqrt(ms + EPS) * gn
    return y * (1.0 + sc) + sh


def _split_bf16(a):
    hi = a.astype(BF16)
    lo = (a - hi.astype(F32)).astype(BF16)
    return hi, lo


def _dot(a, b):
    return jnp.dot(a, b, preferred_element_type=F32)


def _ada_kernel(c_ref, w_ref, b_ref, o_ref):
    c = c_ref[...]
    a = c * jax.nn.sigmoid(c)
    a_hi, a_lo = _split_bf16(a)
    w_hi, w_lo = _split_bf16(w_ref[0])
    o_ref[0] = _dot(a_hi, w_hi) + _dot(a_lo, w_hi) + _dot(a_hi, w_lo) + b_ref[0]


def _ada(c_all, ada_w, ada_b):
    nl, d, n6 = ada_w.shape
    nb = c_all.shape[0]
    tn = 1024
    return pl.pallas_call(
        _ada_kernel,
        grid=(nl, n6 // tn),
        in_specs=[
            pl.BlockSpec((nb, d), lambda l, j: (0, 0)),
            pl.BlockSpec((1, d, tn), lambda l, j: (l, 0, j)),
            pl.BlockSpec((1, 1, tn), lambda l, j: (l, 0, j)),
        ],
        out_specs=pl.BlockSpec((1, nb, tn), lambda l, j: (l, 0, j)),
        out_shape=jax.ShapeDtypeStruct((nl, nb, n6), F32),
        name="ada_mod",
    )(c_all, ada_w, ada_b.reshape(nl, 1, n6))


def _lin_scan(a, b, h_in, tt):
    sub = lax.broadcasted_iota(I32, a.shape, 0) & (SUBLANES - 1)
    s = 1
    while s < SUBLANES:
        keep = sub >= s
        a_sh = jnp.where(keep, pltpu.roll(a, s, 0), 1.0)
        b_sh = jnp.where(keep, pltpu.roll(b, s, 0), 0.0)
        b = a * b_sh + b
        a = a * a_sh
        s *= 2
    out, carry = [], h_in
    for v in range(tt // SUBLANES):
        rows = slice(v * SUBLANES, (v + 1) * SUBLANES)
        hv = b[rows] + a[rows] * carry
        carry = hv[SUBLANES - 1:SUBLANES]
        out.append(hv)
    return jnp.concatenate(out, axis=0)


def _lru_kernel(tt, x_ref, mod_ref, nmix_ref, nffn_ref, win_ref, bin_ref, cw_ref, cb_ref,
                gaw_ref, gab_ref, gxw_ref, gxb_ref, lam_ref, wout_ref, bout_ref, h0_ref, c0_ref,
                x1_ref, h2_ref, newh_ref, newc_ref, hcar, win):
    j = pl.program_id(1)
    nj = pl.num_programs(1)
    hist = SUBLANES - (LRU_CONV_WIDTH - 1)

    @pl.when(j == 0)
    def _():
        hcar[...] = h0_ref[0]
        win[hist:SUBLANES, :] = c0_ref[0]

    x = x_ref[0]
    sh1, sc1, g1 = mod_ref[0, 0:1, :], mod_ref[0, 1:2, :], mod_ref[0, 2:3, :]
    sh2, sc2 = mod_ref[0, 3:4, :], mod_ref[0, 4:5, :]
    h = _rms_mod(x, nmix_ref[...], sh1, sc1)
    proj = _dot(h.astype(BF16), win_ref[...]) + bin_ref[...]
    gate = jax.nn.gelu(proj[:, :LRU_WIDTH])
    win[SUBLANES:SUBLANES + tt, :] = proj[:, LRU_WIDTH:]

    xc = cb_ref[...] + win[pl.ds(hist, tt), :] * cw_ref[0:1, :]
    for k in range(1, LRU_CONV_WIDTH):
        xc = xc + win[pl.ds(hist + k, tt), :] * cw_ref[k:k + 1, :]
    newc = win[tt + hist:tt + SUBLANES, :]
    win[hist:SUBLANES, :] = newc

    xcb = xc.astype(BF16)
    ga = jnp.concatenate(
        [_dot(xcb[:, i * LRU_BLOCK:(i + 1) * LRU_BLOCK], gaw_ref[i]) for i in range(LRU_HEADS)], axis=-1)
    gx = jnp.concatenate(
        [_dot(xcb[:, i * LRU_BLOCK:(i + 1) * LRU_BLOCK], gxw_ref[i]) for i in range(LRU_HEADS)], axis=-1)
    r = jax.nn.sigmoid(ga + gab_ref[...])
    gi = jax.nn.sigmoid(gx + gxb_ref[...])
    nl = -lam_ref[...]
    softplus = jnp.maximum(nl, 0.0) + jnp.log(1.0 + jnp.exp(-jnp.abs(nl)))
    log_a = (-LRU_C) * r * softplus
    a = jnp.exp(log_a)
    mult = jnp.sqrt(1.0 - jnp.exp(2.0 * log_a))
    bv = mult * (gi * xc)
    hs = _lin_scan(a, bv, hcar[...], tt)
    hcar[...] = hs[tt - 1:tt, :]

    out = _dot((hs * gate).astype(BF16), wout_ref[...]) + bout_ref[...]
    x1 = x + g1 * out
    x1_ref[0] = x1
    h2_ref[0] = _rms_mod(x1, nffn_ref[...], sh2, sc2)

    @pl.when(j == nj - 1)
    def _():
        newh_ref[0] = hcar[...]
        newc_ref[0] = newc


def _lru_layer(x, mod, nmix, nffn, p, h0, c0, tt):
    bsz, t, d = x.shape
    w = LRU_WIDTH
    cw = LRU_CONV_WIDTH
    row = lambda b, j: (b, j, 0)
    per_b = lambda b, j: (b, 0, 0)
    c2 = lambda b, j: (0, 0)
    c3 = lambda b, j: (0, 0, 0)
    return pl.pallas_call(
        functools.partial(_lru_kernel, tt),
        grid=(bsz, t // tt),
        in_specs=[
            pl.BlockSpec((1, tt, d), row),
            pl.BlockSpec((1, 6, d), per_b),
            pl.BlockSpec((1, d), c2), pl.BlockSpec((1, d), c2),
            pl.BlockSpec((d, 2 * w), c2), pl.BlockSpec((1, 2 * w), c2),
            pl.BlockSpec((cw, w), c2), pl.BlockSpec((1, w), c2),
            pl.BlockSpec((LRU_HEADS, LRU_BLOCK, LRU_BLOCK), c3), pl.BlockSpec((1, w), c2),
            pl.BlockSpec((LRU_HEADS, LRU_BLOCK, LRU_BLOCK), c3), pl.BlockSpec((1, w), c2),
            pl.BlockSpec((1, w), c2),
            pl.BlockSpec((w, d), c2), pl.BlockSpec((1, d), c2),
            pl.BlockSpec((1, 1, w), per_b), pl.BlockSpec((1, cw - 1, w), per_b),
        ],
        out_specs=[
            pl.BlockSpec((1, tt, d), row), pl.BlockSpec((1, tt, d), row),
            pl.BlockSpec((1, 1, w), per_b), pl.BlockSpec((1, cw - 1, w), per_b),
        ],
        out_shape=[
            jax.ShapeDtypeStruct((bsz, t, d), F32), jax.ShapeDtypeStruct((bsz, t, d), F32),
            jax.ShapeDtypeStruct((bsz, 1, w), F32), jax.ShapeDtypeStruct((bsz, cw - 1, w), F32),
        ],
        scratch_shapes=[pltpu.VMEM((1, w), F32), pltpu.VMEM((SUBLANES + tt, w), F32)],
        compiler_params=pltpu.CompilerParams(
            dimension_semantics=("arbitrary", "arbitrary"), vmem_limit_bytes=VMEM_MIXER_LIMIT),
        name="rglru_mixer",
    )(x, mod, nmix, nffn, p["w_in"], p["b_in"], p["conv_w"], p["conv_b"], p["ga_w"], p["ga_b"],
      p["gx_w"], p["gx_b"], p["lam"], p["w_out"], p["b_out"], h0, c0)


CF_HIST = 32


def _conf_kernel(tt, x_ref, mod_ref, nmix_ref, nffn_ref, w1_ref, b1_ref, dw_ref, db_ref,
                 lng_ref, lnb_ref, w2_ref, b2_ref, d0_ref, x1_ref, h2_ref, newd_ref, win, shf):
    j = pl.program_id(1)
    nj = pl.num_programs(1)
    nh = CF_CONV_WIDTH - 1
    first = CF_HIST - nh

    @pl.when(j == 0)
    def _():
        win[first:CF_HIST, :] = d0_ref[0]

    x = x_ref[0]
    sh1, sc1, g1 = mod_ref[0, 0:1, :], mod_ref[0, 1:2, :], mod_ref[0, 2:3, :]
    sh2, sc2 = mod_ref[0, 3:4, :], mod_ref[0, 4:5, :]
    h = _rms_mod(x, nmix_ref[...], sh1, sc1)
    proj = _dot(h.astype(BF16), w1_ref[...]) + b1_ref[...]
    glu = proj[:, :D_MODEL] * jax.nn.sigmoid(proj[:, D_MODEL:])
    win[CF_HIST:CF_HIST + tt, :] = glu

    for r in range(1, SUBLANES):
        shf[r - 1] = win[pl.ds(r, CF_HIST + tt - SUBLANES), :]
    dcv = db_ref[...]
    for k in range(CF_CONV_WIDTH):
        q, r = divmod(first + k, SUBLANES)
        rows = pl.ds(q * SUBLANES, tt)
        tap = win[rows, :] if r == 0 else shf[r - 1, rows, :]
        dcv = dcv + tap * dw_ref[k:k + 1, :]
    newd = win[tt + first:tt + CF_HIST, :]
    win[first:CF_HIST, :] = newd

    mu = jnp.mean(dcv, axis=-1, keepdims=True)
    cen = dcv - mu
    var = jnp.mean(cen * cen, axis=-1, keepdims=True)
    ln = cen * lax.rsqrt(var + EPS) * lng_ref[...] + lnb_ref[...]
    act = ln * jax.nn.sigmoid(ln)
    out = _dot(act.astype(BF16), w2_ref[...]) + b2_ref[...]
    x1 = x + g1 * out
    x1_ref[0] = x1
    h2_ref[0] = _rms_mod(x1, nffn_ref[...], sh2, sc2)

    @pl.when(j == nj - 1)
    def _():
        newd_ref[0] = newd


def _conf_layer(x, mod, nmix, nffn, p, d0, tt):
    bsz, t, d = x.shape
    nh = CF_CONV_WIDTH - 1
    row = lambda b, j: (b, j, 0)
    per_b = lambda b, j: (b, 0, 0)
    c2 = lambda b, j: (0, 0)
    return pl.pallas_call(
        functools.partial(_conf_kernel, tt),
        grid=(bsz, t // tt),
        in_specs=[
            pl.BlockSpec((1, tt, d), row),
            pl.BlockSpec((1, 6, d), per_b),
            pl.BlockSpec((1, d), c2), pl.BlockSpec((1, d), c2),
            pl.BlockSpec((d, 2 * d), c2), pl.BlockSpec((1, 2 * d), c2),
            pl.BlockSpec((CF_CONV_WIDTH, d), c2), pl.BlockSpec((1, d), c2),
            pl.BlockSpec((1, d), c2), pl.BlockSpec((1, d), c2),
            pl.BlockSpec((d, d), c2), pl.BlockSpec((1, d), c2),
            pl.BlockSpec((1, nh, d), per_b),
        ],
        out_specs=[
            pl.BlockSpec((1, tt, d), row), pl.BlockSpec((1, tt, d), row),
            pl.BlockSpec((1, nh, d), per_b),
        ],
        out_shape=[
            jax.ShapeDtypeStruct((bsz, t, d), F32), jax.ShapeDtypeStruct((bsz, t, d), F32),
            jax.ShapeDtypeStruct((bsz, nh, d), F32),
        ],
        scratch_shapes=[pltpu.VMEM((CF_HIST + tt, d), F32),
                        pltpu.VMEM((SUBLANES - 1, CF_HIST + tt - SUBLANES, d), F32)],
        compiler_params=pltpu.CompilerParams(
            dimension_semantics=("arbitrary", "arbitrary"), vmem_limit_bytes=VMEM_MIXER_LIMIT),
        name="conformer_mixer",
    )(x, mod, nmix, nffn, p["w_pw1"], p["b_pw1"], p["dw_w"], p["dw_b"], p["ln_g"], p["ln_b"],
      p["w_pw2"], p["b_pw2"], d0)


def _top16_rows(s, payload=None):
    nrows = s.shape[0]
    row = lax.broadcasted_iota(I32, s.shape, 0)
    vals, picked = [], []
    for _ in range(PEER_TOPK):
        m = jnp.max(s, axis=0, keepdims=True)
        idx = jnp.min(jnp.where(s == m, row, nrows), axis=0, keepdims=True)
        hit = row == idx
        vals.append(m)
        if payload is None:
            picked.append(idx)
        else:
            picked.append(jnp.sum(jnp.where(hit, payload, 0), axis=0, keepdims=True))
        s = jnp.where(hit, -jnp.inf, s)
    return jnp.concatenate(vals, axis=0), jnp.concatenate(picked, axis=0)


def _peer_q_kernel(h_ref, wq_ref, k1_ref, k2_ref, e_ref, g_ref):
    half = PEER_DK // 2
    q = _dot(h_ref[...].astype(BF16), wq_ref[...])
    nt = (((1,), (1,)), ((), ()))
    s1 = lax.dot_general(k1_ref[...], q[:, :half].astype(BF16), nt, preferred_element_type=F32)
    s2 = lax.dot_general(k2_ref[...], q[:, half:].astype(BF16), nt, preferred_element_type=F32)
    s1v, s1i = _top16_rows(s1)
    s2v, s2i = _top16_rows(s2)
    cand, cand_e = [], []
    for a in range(PEER_TOPK):
        nb = PEER_TOPK // (a + 1)
        cand.append(s1v[a:a + 1, :] + s2v[:nb, :])
        cand_e.append(s1i[a:a + 1, :] * PEER_NKEYS + s2i[:nb, :])
    n = sum(c.shape[0] for c in cand)
    pad = -n % SUBLANES
    cand.append(jnp.full((pad, s1v.shape[1]), -jnp.inf, F32))
    cand_e.append(jnp.zeros((pad, s1v.shape[1]), I32))
    sv, e = _top16_rows(jnp.concatenate(cand, axis=0), jnp.concatenate(cand_e, axis=0))
    e_ref[0] = e
    ex = jnp.exp(sv - sv[0:1, :])
    g_ref[0] = ex / jnp.sum(ex, axis=0, keepdims=True)


def _peer_q(h2, wq, k1, k2, tq):
    ntok, d = h2.shape
    return pl.pallas_call(
        _peer_q_kernel,
        grid=(ntok // tq, PEER_HEADS),
        in_specs=[
            pl.BlockSpec((tq, d), lambda i, h: (i, 0)),
            pl.BlockSpec((d, PEER_DK), lambda i, h: (0, h)),
            pl.BlockSpec((PEER_NKEYS, PEER_DK // 2), lambda i, h: (0, 0)),
            pl.BlockSpec((PEER_NKEYS, PEER_DK // 2), lambda i, h: (0, 0)),
        ],
        out_specs=[
            pl.BlockSpec((1, PEER_TOPK, tq), lambda i, h: (h, 0, i)),
            pl.BlockSpec((1, PEER_TOPK, tq), lambda i, h: (h, 0, i)),
        ],
        out_shape=[
            jax.ShapeDtypeStruct((PEER_HEADS, PEER_TOPK, ntok), I32),
            jax.ShapeDtypeStruct((PEER_HEADS, PEER_TOPK, ntok), F32),
        ],
        compiler_params=pltpu.CompilerParams(dimension_semantics=("arbitrary", "arbitrary")),
        name="peer_query_topk",
    )(h2, wq, k1, k2)


def _pack_table(tab):
    lo = lax.bitcast_convert_type(tab[:, :HALF_WORDS].astype(BF16), jnp.uint16).astype(jnp.uint32)
    hi = lax.bitcast_convert_type(tab[:, HALF_WORDS:].astype(BF16), jnp.uint16).astype(jnp.uint32)
    words = lax.bitcast_convert_type(lo | (hi << 16), I32)
    return words.reshape(tab.shape[0], HALF_SUB, LANES)


def _pair_pattern(nrows):
    probe = pltpu.bitcast(jnp.full((SUBLANES, LANES), 0x3F800000, I32), BF16)
    odd_is_hi = probe[1:2, 0:1].astype(F32) == 1.0
    m = lax.broadcasted_iota(I32, (nrows, STAGE_ROWS), 0) & (SUBLANES - 1)
    r = lax.broadcasted_iota(I32, (nrows, STAGE_ROWS), 1)
    chunk = (r >> 1) & (HALF_SUB - 1)
    par = r & 1
    half = jnp.where(odd_is_hi, par, 1 - par)
    return m == half * HALF_SUB + chunk


def _stage_rows(e_ref, t, tab_ref, stage_ref):
    for j in range(0, PEER_PICKS, 2):
        pair = jnp.concatenate([tab_ref[e_ref[t, j]], tab_ref[e_ref[t, j + 1]]], axis=0)
        stage_ref[j * HALF_SUB:(j + 2) * HALF_SUB, :] = pair
    return pltpu.bitcast(stage_ref[...], BF16)


def _pick_of_stage_row(shape, pick_axis):
    r = lax.broadcasted_iota(I32, shape, 1 - pick_axis)
    j = lax.broadcasted_iota(I32, shape, pick_axis)
    return jnp.where((r >> 3) == j, 1.0, 0.0).astype(BF16)


def _peer_u_kernel(tt, e_ref, h_ref, g_ref, tab_ref, w_ref, stage0, stage1, zs_ref):
    mask = _pair_pattern(2 * SUBLANES)
    nt = (((1,), (1,)), ((), ()))
    stages = (stage0, stage1)

    def body(tg, carry):
        base = pl.multiple_of(tg * GROUP, GROUP)
        rows = []
        for i in range(GROUP):
            t = base + i
            xb = _stage_rows(e_ref, t, tab_ref, stages[i % 2])
            h_hi, h_lo = _split_bf16(h_ref[t])
            h16 = jnp.concatenate([h_hi, h_lo], axis=0)
            out = lax.dot_general(h16, xb, nt, preferred_element_type=F32)
            m = jnp.where(mask, out, 0.0)
            rows.append(jnp.sum(m[:SUBLANES] + m[SUBLANES:], axis=0, keepdims=True))
        zs_ref[pl.ds(base, GROUP), :] = jnp.concatenate(rows, axis=0)
        return carry

    lax.fori_loop(0, tt // GROUP, body, 0)
    z_hi, z_lo = _split_bf16(zs_ref[...])
    fold = _pick_of_stage_row((STAGE_ROWS, PEER_PICKS), 1)
    z = _dot(z_hi, fold) + _dot(z_lo, fold)
    w_ref[...] = g_ref[...] * jax.nn.gelu(z)


def _peer_u(e, h2, g, tab, tt):
    ntok = e.shape[0]
    return pl.pallas_call(
        functools.partial(_peer_u_kernel, tt),
        grid=(ntok // tt,),
        in_specs=[
            pl.BlockSpec((tt, PEER_PICKS), lambda i: (i, 0), memory_space=pltpu.SMEM),
            pl.BlockSpec((tt, SUBLANES, LANES), lambda i: (i, 0, 0)),
            pl.BlockSpec((tt, PEER_PICKS), lambda i: (i, 0)),
            pl.BlockSpec((PEER_NEXPERTS, HALF_SUB, LANES), lambda i: (0, 0, 0),
                         pipeline_mode=pl.Buffered(1)),
        ],
        out_specs=pl.BlockSpec((tt, PEER_PICKS), lambda i: (i, 0)),
        out_shape=jax.ShapeDtypeStruct((ntok, PEER_PICKS), F32),
        scratch_shapes=[pltpu.VMEM((PEER_PICKS * HALF_SUB, LANES), I32),
                        pltpu.VMEM((PEER_PICKS * HALF_SUB, LANES), I32),
                        pltpu.VMEM((tt, STAGE_ROWS), F32)],
        compiler_params=pltpu.CompilerParams(
            dimension_semantics=("arbitrary",), vmem_limit_bytes=VMEM_TABLE_LIMIT),
        name="peer_down_gather",
    )(e, h2.reshape(ntok, SUBLANES, LANES), g, tab)


def _peer_v_kernel(tt, final, e_ref, w_ref, x_ref, g2_ref, nf_ref, tab_ref, o_ref, stage0, stage1, wrep_ref):
    mask = _pair_pattern(SUBLANES)
    stages = (stage0, stage1)
    w = w_ref[...]
    w_hi, w_lo = _split_bf16(w)
    w_lo2 = (w - w_hi.astype(F32) - w_lo.astype(F32)).astype(BF16)
    rep = _pick_of_stage_row((PEER_PICKS, STAGE_ROWS), 0)
    wrep_ref[...] = _dot(w_hi, rep) + _dot(w_lo, rep) + _dot(w_lo2, rep)
    ones = jnp.ones((LANES, LANES), BF16)

    def body(tg, carry):
        base = pl.multiple_of(tg * GROUP, GROUP)
        wr8 = wrep_ref[pl.ds(base, GROUP), :]
        xs = []
        for i in range(GROUP):
            t = base + i
            xb = _stage_rows(e_ref, t, tab_ref, stages[i % 2])
            l_hi, l_lo = _split_bf16(jnp.where(mask, wr8[i:i + 1, :], 0.0))
            out = _dot(jnp.concatenate([l_hi, l_lo], axis=0), xb)
            x2 = x_ref[t] + g2_ref[0] * (out[:SUBLANES] + out[SUBLANES:])
            if final:
                xs.append(x2)
            else:
                o_ref[t] = x2
        if final:
            sq = jnp.concatenate([jnp.sum(x2 * x2, axis=0, keepdims=True) for x2 in xs], axis=0)
            s_hi, s_lo = _split_bf16(sq)
            s_lo2 = (sq - s_hi.astype(F32) - s_lo.astype(F32)).astype(BF16)
            tot = _dot(s_hi, ones) + _dot(s_lo, ones) + _dot(s_lo2, ones)
            scale = lax.rsqrt(tot * (1.0 / D_MODEL) + EPS)
            for i in range(GROUP):
                o_ref[base + i] = xs[i] * scale[i:i + 1, :] * nf_ref[...]
        return carry

    lax.fori_loop(0, tt // GROUP, body, 0)


def _peer_v(e, w, x1, g2, nfinal, tab, tt, final):
    bsz, t, d = x1.shape
    ntok = bsz * t
    per_seq = t // tt
    return pl.pallas_call(
        functools.partial(_peer_v_kernel, tt, final),
        grid=(ntok // tt,),
        in_specs=[
            pl.BlockSpec((tt, PEER_PICKS), lambda i: (i, 0), memory_space=pltpu.SMEM),
            pl.BlockSpec((tt, PEER_PICKS), lambda i: (i, 0)),
            pl.BlockSpec((tt, SUBLANES, LANES), lambda i: (i, 0, 0)),
            pl.BlockSpec((1, SUBLANES, LANES), lambda i: (i // per_seq, 0, 0)),
            pl.BlockSpec((SUBLANES, LANES), lambda i: (0, 0)),
            pl.BlockSpec((PEER_NEXPERTS, HALF_SUB, LANES), lambda i: (0, 0, 0),
                         pipeline_mode=pl.Buffered(1)),
        ],
        out_specs=pl.BlockSpec((tt, SUBLANES, LANES), lambda i: (i, 0, 0)),
        out_shape=jax.ShapeDtypeStruct((ntok, SUBLANES, LANES), F32),
        scratch_shapes=[pltpu.VMEM((PEER_PICKS * HALF_SUB, LANES), I32),
                        pltpu.VMEM((PEER_PICKS * HALF_SUB, LANES), I32),
                        pltpu.VMEM((tt, STAGE_ROWS), F32)],
        compiler_params=pltpu.CompilerParams(
            dimension_semantics=("arbitrary",), vmem_limit_bytes=VMEM_TABLE_LIMIT),
        name="peer_up_gather",
    )(e, w, x1.reshape(ntok, SUBLANES, LANES), g2.reshape(bsz, SUBLANES, LANES),
      nfinal.reshape(SUBLANES, LANES), tab).reshape(bsz, t, d)


def _peer_ffn(x1, h2, g2, nfinal, pp, final):
    bsz, t, d = x1.shape
    ntok = bsz * t
    tq = 512
    et, gt = _peer_q(h2.reshape(ntok, d), pp["w_q"], pp["k1"], pp["k2"], tq)
    e = et.transpose(2, 0, 1).reshape(ntok, PEER_PICKS)
    g = gt.transpose(2, 0, 1).reshape(ntok, PEER_PICKS)
    w = _peer_u(e, h2.reshape(ntok, d), g, pp["u"], min(ntok, 256))
    return _peer_v(e, w, x1, g2, nfinal, pp["v"], min(t, 256), final)


def _trunk(x, mod, lru_h, lru_conv, dwconv, prm):
    bsz, t, _ = x.shape
    tt = min(t, 256)
    m0, m1 = mod[0], mod[1]
    x1, h2, new_h, new_conv = _lru_layer(x, m0, prm["norm_mix"][0], prm["norm_ffn"][0], prm["lru"],
                                         lru_h, lru_conv, tt)
    x2 = _peer_ffn(x1, h2, m0[:, 5], prm["norm_final"], prm["peer"][0], False)
    x3, h4, new_dw = _conf_layer(x2, m1, prm["norm_mix"][1], prm["norm_ffn"][1], prm["cf"], dwconv, tt)
    y = _peer_ffn(x3, h4, m1[:, 5], prm["norm_final"], prm["peer"][1], True)
    return y, new_h, new_conv, new_dw


def kernel(x_prompt, x_sample, c_prompt, c_sample, state_lru_h, state_lru_conv, state_dwconv, ada_w, ada_b, norm_mix, norm_ffn, norm_final, lru_w_in, lru_b_in, lru_conv_w, lru_conv_b, lru_gate_a_w, lru_gate_a_b, lru_gate_x_w, lru_gate_x_b, lru_lambda, lru_w_out, lru_b_out, cf_w_pw1, cf_b_pw1, cf_dw_w, cf_dw_b, cf_ln_g, cf_ln_b, cf_w_pw2, cf_b_pw2, peer_w_q, peer_k1, peer_k2, peer_u, peer_v):
    bp, bs = x_prompt.shape[0], x_sample.shape[0]
    depth = ada_w.shape[0]
    d = D_MODEL
    row = lambda a: a.reshape(1, -1)
    prm = dict(
        norm_mix=[row(norm_mix[l]) for l in range(depth)],
        norm_ffn=[row(norm_ffn[l]) for l in range(depth)],
        norm_final=norm_final,
        lru=dict(w_in=lru_w_in[0].astype(BF16), b_in=row(lru_b_in[0]), conv_w=lru_conv_w[0],
                 conv_b=row(lru_conv_b[0]), ga_w=lru_gate_a_w[0].astype(BF16), ga_b=row(lru_gate_a_b[0]),
                 gx_w=lru_gate_x_w[0].astype(BF16), gx_b=row(lru_gate_x_b[0]), lam=row(lru_lambda[0]),
                 w_out=lru_w_out[0].astype(BF16), b_out=row(lru_b_out[0])),
        cf=dict(w_pw1=cf_w_pw1[0].astype(BF16), b_pw1=row(cf_b_pw1[0]), dw_w=cf_dw_w[0],
                dw_b=row(cf_dw_b[0]), ln_g=row(cf_ln_g[0]), ln_b=row(cf_ln_b[0]),
                w_pw2=cf_w_pw2[0].astype(BF16), b_pw2=row(cf_b_pw2[0])),
        peer=[dict(w_q=peer_w_q[l].astype(BF16), k1=peer_k1[l].astype(BF16), k2=peer_k2[l].astype(BF16),
                   u=_pack_table(peer_u[l]), v=_pack_table(peer_v[l])) for l in range(depth)],
    )
    mod = _ada(jnp.concatenate([c_prompt, c_sample], axis=0), ada_w, ada_b)
    mod = mod.reshape(depth, bp + bs, 6, d)
    dt = x_prompt.dtype
    zero_h = jnp.zeros((bp, 1, LRU_WIDTH), dt)
    zero_c = jnp.zeros((bp, LRU_CONV_WIDTH - 1, LRU_WIDTH), dt)
    zero_d = jnp.zeros((bp, CF_CONV_WIDTH - 1, d), dt)
    y_p, h_p, c_p, d_p = _trunk(x_prompt, mod[:, :bp], zero_h, zero_c, zero_d, prm)
    y_s, h_s, c_s, d_s = _trunk(x_sample, mod[:, bp:], state_lru_h[0][:, None, :], state_lru_conv[0],
                                state_dwconv[0], prm)
    return (y_p, y_s, h_p.reshape(1, bp, LRU_WIDTH), c_p[None], d_p[None],
            h_s.reshape(1, bs, LRU_WIDTH), c_s[None], d_s[None])
```
